```python
import math
import jax, jax.numpy as jnp
from jax import lax
import numpy as np

D_MODEL = 1024
BATCH = 32
SEQ = 2048
DEPTH = 2
DEC_BATCH = 16
DEC_SEQ = 4096
PAST_LEN = 128

MIX_WIDTH = D_MODEL
CONV_CH = MIX_WIDTH // 2
RET_HEADS = 8
HEAD_DIM = 64
RET_WIDTH = RET_HEADS * HEAD_DIM
CONV_WIDTH = 31
CONV_PAD = CONV_WIDTH // 2
CHUNK = 128
ROPE_BASE = 10000.0
N_EXPERTS = 16
CAP_FACTOR = 2
EXPERT_FF = 2 * D_MODEL
PLE_DIM = 256
EPS = 1e-6
IN_COLS = 2 * CONV_CH + 4 * RET_WIDTH
SPLITS = (2 * CONV_CH, 2 * CONV_CH + RET_WIDTH, 2 * CONV_CH + 2 * RET_WIDTH, 2 * CONV_CH + 3 * RET_WIDTH)

kernel_name = "hymba_conformer_retention_ec_moe_encoder"


def rmsnorm(x, g):
    xf = x.astype(jnp.float32)
    y = xf * lax.rsqrt(jnp.mean(xf * xf, axis=-1, keepdims=True) + EPS)
    return (y * g.astype(jnp.float32)).astype(x.dtype)


def layernorm(x, g, b):
    xf = x.astype(jnp.float32)
    mu = jnp.mean(xf, axis=-1, keepdims=True)
    var = jnp.mean(jnp.square(xf - mu), axis=-1, keepdims=True)
    y = (xf - mu) * lax.rsqrt(var + EPS)
    return (y * g.astype(jnp.float32) + b.astype(jnp.float32)).astype(x.dtype)


def rope(x):
    L = x.shape[1]
    half = HEAD_DIM // 2
    inv = 1.0 / (ROPE_BASE ** (jnp.arange(half, dtype=jnp.float32) / half))
    ang = jnp.arange(L, dtype=jnp.float32)[:, None] * inv[None, :]
    cos = jnp.cos(ang)[None, :, None, :]
    sin = jnp.sin(ang)[None, :, None, :]
    xf = x.astype(jnp.float32)
    x1, x2 = xf[..., :half], xf[..., half:]
    return jnp.concatenate([x1 * cos - x2 * sin, x1 * sin + x2 * cos], axis=-1)


def conformer_conv(u, conv_w, conv_b, ln_g, ln_b):
    a, g = u[..., :CONV_CH], u[..., CONV_CH:]
    h = a * jax.nn.sigmoid(g)
    h = lax.conv_general_dilated(
        h, conv_w[:, None, :].astype(h.dtype), window_strides=(1,),
        padding=[(CONV_PAD, CONV_PAD)], dimension_numbers=("NWC", "WIO", "NWC"),
        feature_group_count=CONV_CH) + conv_b
    h = layernorm(h, ln_g, ln_b)
    return jax.nn.silu(h)


def retention_scan(q, k, v, log_gamma, strict):
    B_, H, L, Dh = q.shape
    nc = L // CHUNK
    def chunks(t):
        return t.reshape(B_, H, nc, CHUNK, Dh).transpose(2, 0, 1, 3, 4)
    lg = log_gamma.astype(jnp.float32)
    idx = jnp.arange(CHUNK, dtype=jnp.float32)
    diff = idx[:, None] - idx[None, :]
    mask = (diff > 0) if strict else (diff >= 0)
    decay_intra = jnp.where(mask[None], jnp.exp(lg[:, None, None] * jnp.where(mask, diff, 0.0)[None]), 0.0)
    xi = jnp.exp(lg[:, None] * (idx[None, :] + 1.0))
    zeta = jnp.exp(lg[:, None] * (CHUNK - 1.0 - idx[None, :]))
    g_chunk = jnp.exp(lg * CHUNK)

    def step(S, inp):
        qi, ki, vi = inp
        s = jnp.einsum("bhid,bhjd->bhij", qi, ki) * decay_intra[None]
        o = (jnp.einsum("bhij,bhjd->bhid", s, vi)
             + jnp.einsum("bhid,bhde->bhie", qi, S) * xi[None, :, :, None])
        S = S * g_chunk[None, :, None, None] + jnp.einsum(
            "bhjd,bhje->bhde", ki * zeta[None, :, :, None], vi)
        return S, o

    S0 = jnp.zeros((B_, H, Dh, Dh), jnp.float32)
    _, o = lax.scan(step, S0, (chunks(q), chunks(k), chunks(v)))
    return o.transpose(1, 2, 0, 3, 4).reshape(B_, H, L, Dh)


def bidirectional_retention(q, k, v, gate, lg_fwd, lg_bwd, gn_g):
    B_, L, _ = q.shape
    qh = rope(q.reshape(B_, L, RET_HEADS, HEAD_DIM)).transpose(0, 2, 1, 3)
    kh = (rope(k.reshape(B_, L, RET_HEADS, HEAD_DIM)) * (HEAD_DIM ** -0.5)).transpose(0, 2, 1, 3)
    vh = v.astype(jnp.float32).reshape(B_, L, RET_HEADS, HEAD_DIM).transpose(0, 2, 1, 3)
    o_f = retention_scan(qh, kh, vh, lg_fwd, strict=False)
    o_b = jnp.flip(retention_scan(jnp.flip(qh, 2), jnp.flip(kh, 2), jnp.flip(vh, 2), lg_bwd, strict=True), 2)
    o = (o_f + o_b).transpose(0, 2, 1, 3)
    mu = jnp.mean(o, axis=-1, keepdims=True)
    var = jnp.mean(jnp.square(o - mu), axis=-1, keepdims=True)
    o = ((o - mu) * lax.rsqrt(var + EPS)).reshape(B_, L, RET_WIDTH) * gn_g.astype(jnp.float32)
    return (jax.nn.silu(gate.astype(jnp.float32)) * o).astype(q.dtype)


def expert_choice_ffn(h, w_router, w_gate, w_up, w_down):
    B_, L, D = h.shape
    T = B_ * L
    cap = CAP_FACTOR * T // N_EXPERTS
    xt = h.reshape(T, D)
    aff = jax.nn.softmax((xt @ w_router).astype(jnp.float32), axis=-1)
    g, idx = lax.top_k(aff.T, cap)
    xe = xt[idx]
    hid = jax.nn.silu(jnp.einsum("ecd,edf->ecf", xe, w_gate)) * jnp.einsum("ecd,edf->ecf", xe, w_up)
    ye = jnp.einsum("ecf,efd->ecd", hid, w_down) * g[..., None].astype(h.dtype)
    out = jnp.zeros_like(xt).at[idx.reshape(-1)].add(ye.reshape(-1, D))
    return out.reshape(B_, L, D)


def trunk(h, p, norm_mix, w_in, conv_w, conv_b, conv_ln_g, conv_ln_b, ret_log_gamma_fwd,
          ret_log_gamma_bwd, ret_gn_g, w_out, norm_ffn, w_router, w_exp_gate, w_exp_up,
          w_exp_down, norm_ple, w_ple_gate, w_ple_proj, norm_final):
    for i in range(DEPTH):
        x = rmsnorm(h, norm_mix[i])
        proj = x @ w_in[i]
        u_conv = proj[..., :SPLITS[0]]
        q = proj[..., SPLITS[0]:SPLITS[1]]
        k = proj[..., SPLITS[1]:SPLITS[2]]
        v = proj[..., SPLITS[2]:SPLITS[3]]
        og = proj[..., SPLITS[3]:]
        a_out = conformer_conv(u_conv, conv_w[i], conv_b[i], conv_ln_g[i], conv_ln_b[i])
        b_out = bidirectional_retention(q, k, v, og, ret_log_gamma_fwd[i], ret_log_gamma_bwd[i], ret_gn_g[i])
        h = h + jnp.concatenate([a_out, b_out], axis=-1) @ w_out[i]
        h = h + expert_choice_ffn(rmsnorm(h, norm_ffn[i]), w_router[i], w_exp_gate[i], w_exp_up[i], w_exp_down[i])
        gate = jax.nn.sigmoid(rmsnorm(h, norm_ple[i]) @ w_ple_gate[i])
        h = h + gate * (p[i] @ w_ple_proj[i])
    return rmsnorm(h, norm_final)


def setup_inputs(seed: int = 0) -> dict:
    key = jax.random.key(seed)
    ks = jax.random.split(key, 24)
    f32 = jnp.float32
    def nrm(k, shape, scale):
        return jax.random.normal(k, shape, f32) * scale
    def decays(k, base):
        e = base + jnp.arange(RET_HEADS, dtype=f32)[None, :] + 0.1 * jax.random.normal(k, (DEPTH, RET_HEADS), f32)
        return jnp.log1p(-jnp.exp2(-e))
    return {
        "x_prompt": nrm(ks[0], (BATCH, SEQ, D_MODEL), 1.0),
        "x_sample": nrm(ks[1], (DEC_BATCH, DEC_SEQ, D_MODEL), 1.0),
        "p_prompt": nrm(ks[2], (DEPTH, BATCH, SEQ, PLE_DIM), 1.0),
        "p_sample": nrm(ks[3], (DEPTH, DEC_BATCH, DEC_SEQ, PLE_DIM), 1.0),
        "norm_mix": 1.0 + nrm(ks[4], (DEPTH, D_MODEL), 0.02),
        "w_in": nrm(ks[5], (DEPTH, D_MODEL, IN_COLS), D_MODEL ** -0.5),
        "conv_w": nrm(ks[6], (DEPTH, CONV_WIDTH, CONV_CH), CONV_WIDTH ** -0.5),
        "conv_b": nrm(ks[7], (DEPTH, CONV_CH), 0.02),
        "conv_ln_g": 1.0 + nrm(ks[8], (DEPTH, CONV_CH), 0.02),
        "conv_ln_b": nrm(ks[9], (DEPTH, CONV_CH), 0.02),
        "ret_log_gamma_fwd": decays(ks[10], 5.0),
        "ret_log_gamma_bwd": decays(ks[11], 5.5),
        "ret_gn_g": 1.0 + nrm(ks[12], (DEPTH, RET_WIDTH), 0.02),
        "w_out": nrm(ks[13], (DEPTH, MIX_WIDTH, D_MODEL), MIX_WIDTH ** -0.5 * 0.5),
        "norm_ffn": 1.0 + nrm(ks[14], (DEPTH, D_MODEL), 0.02),
        "w_router": nrm(ks[15], (DEPTH, D_MODEL, N_EXPERTS), D_MODEL ** -0.5),
        "w_exp_gate": nrm(ks[16], (DEPTH, N_EXPERTS, D_MODEL, EXPERT_FF), D_MODEL ** -0.5),
        "w_exp_up": nrm(ks[17], (DEPTH, N_EXPERTS, D_MODEL, EXPERT_FF), D_MODEL ** -0.5),
        "w_exp_down": nrm(ks[18], (DEPTH, N_EXPERTS, EXPERT_FF, D_MODEL), EXPERT_FF ** -0.5),
        "norm_ple": 1.0 + nrm(ks[19], (DEPTH, D_MODEL), 0.02),
        "w_ple_gate": nrm(ks[20], (DEPTH, D_MODEL, D_MODEL), D_MODEL ** -0.5),
        "w_ple_proj": nrm(ks[21], (DEPTH, PLE_DIM, D_MODEL), PLE_DIM ** -0.5 * 0.5),
        "norm_final": 1.0 + nrm(ks[22], (D_MODEL,), 0.02),
    }


def reference(x_prompt, x_sample, p_prompt, p_sample, norm_mix, w_in, conv_w, conv_b, conv_ln_g,
              conv_ln_b, ret_log_gamma_fwd, ret_log_gamma_bwd, ret_gn_g, w_out, norm_ffn, w_router,
              w_exp_gate, w_exp_up, w_exp_down, norm_ple, w_ple_gate, w_ple_proj, norm_final):
    y_prompt = trunk(x_prompt, p_prompt, norm_mix, w_in, conv_w, conv_b, conv_ln_g, conv_ln_b,
                     ret_log_gamma_fwd, ret_log_gamma_bwd, ret_gn_g, w_out, norm_ffn, w_router,
                     w_exp_gate, w_exp_up, w_exp_down, norm_ple, w_ple_gate, w_ple_proj, norm_final)
    y_sample = trunk(x_sample, p_sample, norm_mix, w_in, conv_w, conv_b, conv_ln_g, conv_ln_b,
                     ret_log_gamma_fwd, ret_log_gamma_bwd, ret_gn_g, w_out, norm_ffn, w_router,
                     w_exp_gate, w_exp_up, w_exp_down, norm_ple, w_ple_gate, w_ple_proj, norm_final)
    return (y_prompt, y_sample)
```

```python
import functools
import math

import jax
import jax.numpy as jnp
from jax import lax
from jax.experimental import pallas as pl
from jax.experimental.pallas import tpu as pltpu

D_MODEL = 1024
CONV_CH = 512
RET_HEADS = 8
HEAD_DIM = 64
RET_WIDTH = RET_HEADS * HEAD_DIM
CONV_WIDTH = 31
CONV_PAD = CONV_WIDTH // 2
CHUNK = 128
ROPE_BASE = 10000.0
N_EXPERTS = 16
CAP_FACTOR = 2
EXPERT_FF = 2 * D_MODEL
PLE_DIM = 256
EPS = 1e-6
SPLITS = (2 * CONV_CH, 2 * CONV_CH + RET_WIDTH, 2 * CONV_CH + 2 * RET_WIDTH,
          2 * CONV_CH + 3 * RET_WIDTH, 2 * CONV_CH + 4 * RET_WIDTH)

LANES = 128
SUBLANES = 8
V7X_VMEM_BYTES = 64 * 1024 * 1024
V7X_VMEM_USABLE = 56 * 1024 * 1024

BF16 = jnp.bfloat16
F32 = jnp.float32
NT_DIMS = (((1,), (1,)), ((), ()))
TN_DIMS = (((0,), (0,)), ((), ()))


def _vmem_limit(estimate_bytes):
    return int(min(V7X_VMEM_USABLE, max(16 * 1024 * 1024, estimate_bytes)))


def _token_tile(seq_len, want=512):
    tm = want
    while seq_len % tm:
        tm //= 2
    return tm


def _rmsnorm(x, g):
    y = x * lax.rsqrt(jnp.mean(x * x, axis=-1, keepdims=True) + EPS)
    return y * g


def _silu(x):
    return x * (1.0 / (1.0 + jnp.exp(-x)))


def _sigmoid(x):
    return 1.0 / (1.0 + jnp.exp(-x))


def _in_proj_kernel(h_ref, g_ref, w_ref, cq_ref, sq_ref, u_ref, q_ref, k_ref, v_ref, og_ref):
    xn = _rmsnorm(h_ref[...], g_ref[...]).astype(BF16)

    def seg(lo, hi):
        return jnp.dot(xn, w_ref[:, lo:hi], preferred_element_type=F32)

    u_ref[...] = seg(0, SPLITS[0]).astype(BF16)
    cos = cq_ref[...]
    sin = sq_ref[...]
    scale = HEAD_DIM ** -0.5
    for lo, ref, mul in ((SPLITS[0], q_ref, 1.0), (SPLITS[1], k_ref, scale)):
        for t in range(RET_WIDTH // LANES):
            x = seg(lo + t * LANES, lo + (t + 1) * LANES)
            r = x * cos + pltpu.roll(x, LANES // 2, 1) * sin
            if mul != 1.0:
                r = r * mul
            ref[:, t * LANES:(t + 1) * LANES] = r.astype(BF16)
    v_ref[...] = seg(SPLITS[2], SPLITS[3]).astype(BF16)
    og_ref[...] = seg(SPLITS[3], SPLITS[4]).astype(BF16)


def _in_proj(h, g, w, cos_t, sin_t, seq_len):
    T = h.shape[0]
    tm = _token_tile(seq_len)
    nl = seq_len // tm
    row = lambda i: (i, 0)
    est = 2 * (tm * D_MODEL * 4 + tm * SPLITS[4] * 2 + 2 * tm * LANES * 4) + 2 * D_MODEL * SPLITS[4] * 2 \
        + 6 * tm * D_MODEL * 4
    return pl.pallas_call(
        _in_proj_kernel,
        grid=(T // tm,),
        in_specs=[
            pl.BlockSpec((tm, D_MODEL), row),
            pl.BlockSpec((1, D_MODEL), lambda i: (0, 0)),
            pl.BlockSpec((D_MODEL, SPLITS[4]), lambda i: (0, 0)),
            pl.BlockSpec((tm, LANES), lambda i: (i % nl, 0)),
            pl.BlockSpec((tm, LANES), lambda i: (i % nl, 0)),
        ],
        out_specs=[
            pl.BlockSpec((tm, 2 * CONV_CH), row),
            pl.BlockSpec((tm, RET_WIDTH), row),
            pl.BlockSpec((tm, RET_WIDTH), row),
            pl.BlockSpec((tm, RET_WIDTH), row),
            pl.BlockSpec((tm, RET_WIDTH), row),
        ],
        out_shape=[
            jax.ShapeDtypeStruct((T, 2 * CONV_CH), BF16),
            jax.ShapeDtypeStruct((T, RET_WIDTH), BF16),
            jax.ShapeDtypeStruct((T, RET_WIDTH), BF16),
            jax.ShapeDtypeStruct((T, RET_WIDTH), BF16),
            jax.ShapeDtypeStruct((T, RET_WIDTH), BF16),
        ],
        compiler_params=pltpu.CompilerParams(
            dimension_semantics=("parallel",), vmem_limit_bytes=_vmem_limit(est)),
        name="in_proj",
    )(h, g, w, cos_t, sin_t)


CONV_ROWS = 128
CONV_HALO = 16
CONV_WIN = CONV_ROWS + 2 * CONV_HALO


def _conv_kernel(u_ref, w_ref, b_ref, lg_ref, lb_ref, o_ref, hp_ref, cv_ref, *, seq_len):
    n_chunks = seq_len // CONV_ROWS
    zeros = jnp.zeros((CONV_HALO, CONV_CH), F32)
    hp_ref[0:CONV_HALO, :] = zeros
    hp_ref[CONV_HALO + seq_len:CONV_HALO + seq_len + CONV_HALO, :] = zeros

    def glu(ci, c):
        r0 = pl.multiple_of(ci * CONV_ROWS, CONV_ROWS)
        rows = u_ref[0, pl.ds(r0, CONV_ROWS), :].astype(F32)
        hp_ref[pl.ds(CONV_HALO + r0, CONV_ROWS), :] = rows[:, :CONV_CH] * _sigmoid(rows[:, CONV_CH:])
        return c

    lax.fori_loop(0, n_chunks, glu, 0)

    def conv(ci, c):
        r0 = pl.multiple_of(ci * CONV_ROWS, CONV_ROWS)
        for t in range(CONV_CH // LANES):
            cols = slice(t * LANES, (t + 1) * LANES)
            win = hp_ref[pl.ds(r0, CONV_WIN), cols]
            acc = jnp.zeros((CONV_ROWS, LANES), F32)
            for phase in range(SUBLANES):
                offs = [o for o in range(CONV_HALO - CONV_PAD, CONV_HALO - CONV_PAD + CONV_WIDTH)
                        if o % SUBLANES == phase]
                if not offs:
                    continue
                shifted = win if phase == 0 else pltpu.roll(win, CONV_WIN - phase, 0)
                for o in offs:
                    j = o - (CONV_HALO - CONV_PAD)
                    base = o - phase
                    acc = acc + shifted[base:base + CONV_ROWS] * w_ref[j:j + 1, cols]
            cv_ref[:, cols] = acc + b_ref[:, cols]
        y = cv_ref[...]
        mu = jnp.mean(y, axis=-1, keepdims=True)
        yc = y - mu
        var = jnp.mean(yc * yc, axis=-1, keepdims=True)
        z = yc * lax.rsqrt(var + EPS) * lg_ref[...] + lb_ref[...]
        o_ref[0, pl.ds(r0, CONV_ROWS), :] = _silu(z).astype(BF16)
        return c

    lax.fori_loop(0, n_chunks, conv, 0)


def _conv(u, w, b, lg, lb, batch, seq_len):
    u3 = u.reshape(batch, seq_len, 2 * CONV_CH)
    est = 2 * (seq_len * 2 * CONV_CH * 2 + seq_len * CONV_CH * 2) + (seq_len + 2 * CONV_HALO) * CONV_CH * 4 \
        + 8 * CONV_ROWS * CONV_CH * 4 + (1 << 20)
    const = lambda i: (0, 0)
    out = pl.pallas_call(
        functools.partial(_conv_kernel, seq_len=seq_len),
        grid=(batch,),
        in_specs=[
            pl.BlockSpec((1, seq_len, 2 * CONV_CH), lambda i: (i, 0, 0)),
            pl.BlockSpec((CONV_WIDTH, CONV_CH), const),
            pl.BlockSpec((1, CONV_CH), const),
            pl.BlockSpec((1, CONV_CH), const),
            pl.BlockSpec((1, CONV_CH), const),
        ],
        out_specs=pl.BlockSpec((1, seq_len, CONV_CH), lambda i: (i, 0, 0)),
        out_shape=jax.ShapeDtypeStruct((batch, seq_len, CONV_CH), BF16),
        scratch_shapes=[
            pltpu.VMEM((seq_len + 2 * CONV_HALO, CONV_CH), F32),
            pltpu.VMEM((CONV_ROWS, CONV_CH), F32),
        ],
        compiler_params=pltpu.CompilerParams(
            dimension_semantics=("parallel",), vmem_limit_bytes=_vmem_limit(est)),
        name="conv",
    )(u3, w, b, lg, lb)
    return out.reshape(batch * seq_len, CONV_CH)


def _dot2(x, m):
    hi = x.astype(BF16)
    lo = (x - hi.astype(F32)).astype(BF16)
    return jnp.dot(hi, m, preferred_element_type=F32) + jnp.dot(lo, m, preferred_element_type=F32)


def _retention_kernel(lgf_ref, lgb_ref, q_ref, k_ref, v_ref, og_ref, gn_ref, o_ref, sb_ref, *, seq_len):
    nc = seq_len // CHUNK
    pair = pl.program_id(1)
    lgf = (lgf_ref[2 * pair], lgf_ref[2 * pair + 1])
    lgb = (lgb_ref[2 * pair], lgb_ref[2 * pair + 1])

    lane = lax.broadcasted_iota(jnp.int32, (CHUNK, LANES), 1)
    row = lax.broadcasted_iota(jnp.int32, (CHUNK, LANES), 0)
    rowf = row.astype(F32)
    qk_head1 = ((lane // (HEAD_DIM // 2)) % 2) == 1
    v_head1 = lane >= HEAD_DIM
    krow_head1 = ((row // (HEAD_DIM // 2)) % 2) == 1
    same_head = krow_head1 == v_head1

    def per_lane(pairvals, head1):
        return jnp.where(head1, pairvals[1], pairvals[0])

    lgf_qk = per_lane(lgf, qk_head1)
    lgb_qk = per_lane(lgb, qk_head1)
    lgf_v = per_lane(lgf, v_head1)
    lgb_v = per_lane(lgb, v_head1)
    zeta_f = jnp.exp(lgf_qk * (CHUNK - 1.0 - rowf))
    zeta_b = jnp.exp(lgb_qk * rowf)
    xi_f = jnp.exp(lgf_v * (rowf + 1.0))
    xi_b = jnp.exp(lgb_v * (CHUNK - rowf))
    dg_f = jnp.exp(per_lane(lgf, krow_head1) * float(CHUNK))
    dg_b = jnp.exp(per_lane(lgb, krow_head1) * float(CHUNK))
    diff = (row - lane).astype(F32)
    decay = []
    for hh in range(2):
        fwd = jnp.exp(lgf[hh] * jnp.maximum(diff, 0.0))
        bwd = jnp.exp(lgb[hh] * jnp.maximum(-diff, 0.0))
        decay.append(jnp.where(diff >= 0.0, fwd, bwd))
    decay2 = jnp.concatenate(decay, axis=0)
    group_mean = jnp.where((row >= HEAD_DIM) == v_head1, 1.0 / HEAD_DIM, 0.0).astype(BF16)

    def chunk_rows(c):
        return pl.ds(pl.multiple_of(c * CHUNK, CHUNK), CHUNK)

    def kv_update(kc, vc, zeta):
        kz = (kc.astype(F32) * zeta).astype(BF16)
        upd = lax.dot_general(kz, vc, TN_DIMS, preferred_element_type=F32)
        return jnp.where(same_head, upd, 0.0)

    def bwd_body(i, sb):
        c = nc - 1 - i
        sb_ref[c] = sb
        rows = chunk_rows(c)
        return sb * dg_b + kv_update(k_ref[rows, :], v_ref[rows, :], zeta_b)

    lax.fori_loop(0, nc, bwd_body, jnp.zeros((LANES, LANES), F32))

    def fwd_body(c, sf):
        rows = chunk_rows(c)
        qc = q_ref[rows, :]
        kc = k_ref[rows, :]
        vc = v_ref[rows, :]
        zero = jnp.zeros_like(qc)
        q2 = jnp.concatenate([jnp.where(qk_head1, zero, qc), jnp.where(qk_head1, qc, zero)], axis=0)
        s = lax.dot_general(q2, kc, NT_DIMS, preferred_element_type=F32)
        p = (s * decay2).astype(BF16)
        p2 = jnp.concatenate([p[:CHUNK], p[CHUNK:]], axis=1)
        v2 = jnp.concatenate([jnp.where(v_head1, zero, vc), jnp.where(v_head1, vc, zero)], axis=0)
        o = jnp.dot(p2, v2, preferred_element_type=F32)
        o = o + xi_f * jnp.dot(qc, sf.astype(BF16), preferred_element_type=F32)
        o = o + xi_b * jnp.dot(qc, sb_ref[c].astype(BF16), preferred_element_type=F32)
        mu = _dot2(o, group_mean)
        oc = o - mu
        var = _dot2(oc * oc, group_mean)
        on = oc * lax.rsqrt(var + EPS) * gn_ref[...]
        o_ref[rows, :] = (_silu(og_ref[rows, :].astype(F32)) * on).astype(BF16)
        return sf * dg_f + kv_update(kc, vc, zeta_f)

    lax.fori_loop(0, nc, fwd_body, jnp.zeros((LANES, LANES), F32))


def _retention(q, k, v, og, lgf, lgb, gn, batch, seq_len):
    T = batch * seq_len
    blk = pl.BlockSpec((seq_len, LANES), lambda b, p: (b, p))
    smem = pl.BlockSpec(memory_space=pltpu.SMEM)
    est = 2 * 5 * seq_len * LANES * 2 + (seq_len // CHUNK) * LANES * LANES * 4 + (4 << 20)
    return pl.pallas_call(
        functools.partial(_retention_kernel, seq_len=seq_len),
        grid=(batch, RET_WIDTH // LANES),
        in_specs=[smem, smem, blk, blk, blk, blk, pl.BlockSpec((1, LANES), lambda b, p: (0, p))],
        out_specs=blk,
        out_shape=jax.ShapeDtypeStruct((T, RET_WIDTH), BF16),
        scratch_shapes=[pltpu.VMEM((seq_len // CHUNK, LANES, LANES), F32)],
        compiler_params=pltpu.CompilerParams(
            dimension_semantics=("parallel", "parallel"), vmem_limit_bytes=_vmem_limit(est)),
        name="retention",
    )(lgf, lgb, q, k, v, og, gn)


def _out_proj_kernel(a_ref, b_ref, h_ref, wa_ref, wb_ref, g_ref, wrh_ref, wrl_ref, hn_ref, hacc_ref, aff_ref):
    y = jnp.dot(a_ref[...], wa_ref[...], preferred_element_type=F32)
    y = y + jnp.dot(b_ref[...], wb_ref[...], preferred_element_type=F32)
    hn = h_ref[...] + y
    hn_ref[...] = hn
    hacc_ref[...] = hn
    xn = _rmsnorm(hn, g_ref[...])
    x_hi = xn.astype(BF16)
    x_lo = (xn - x_hi.astype(F32)).astype(BF16)
    wr_hi = wrh_ref[...]
    logits = lax.dot_general(wr_hi, x_hi, NT_DIMS, preferred_element_type=F32)
    logits = logits + lax.dot_general(wr_hi, x_lo, NT_DIMS, preferred_element_type=F32)
    logits = logits + lax.dot_general(wrl_ref[...], x_hi, NT_DIMS, preferred_element_type=F32)
    e = jnp.exp(logits - jnp.max(logits, axis=0, keepdims=True))
    aff_ref[...] = e / jnp.sum(e, axis=0, keepdims=True)


def _out_proj(a, b, h, wa, wb, g, wr_hi, wr_lo, seq_len):
    T = h.shape[0]
    tm = _token_tile(seq_len)
    row = lambda i: (i, 0)
    const = lambda i: (0, 0)
    est = 2 * (2 * tm * CONV_CH * 2 + 3 * tm * D_MODEL * 4 + N_EXPERTS * tm * 4) + 2 * D_MODEL * D_MODEL * 2 \
        + 6 * tm * D_MODEL * 4
    return pl.pallas_call(
        _out_proj_kernel,
        grid=(T // tm,),
        in_specs=[
            pl.BlockSpec((tm, CONV_CH), row),
            pl.BlockSpec((tm, RET_WIDTH), row),
            pl.BlockSpec((tm, D_MODEL), row),
            pl.BlockSpec((CONV_CH, D_MODEL), const),
            pl.BlockSpec((RET_WIDTH, D_MODEL), const),
            pl.BlockSpec((1, D_MODEL), const),
            pl.BlockSpec((N_EXPERTS, D_MODEL), const),
            pl.BlockSpec((N_EXPERTS, D_MODEL), const),
        ],
        out_specs=[
            pl.BlockSpec((tm, D_MODEL), row),
            pl.BlockSpec((tm, D_MODEL), row),
            pl.BlockSpec((N_EXPERTS, tm), lambda i: (0, i)),
        ],
        out_shape=[
            jax.ShapeDtypeStruct((T, D_MODEL), F32),
            jax.ShapeDtypeStruct((T, D_MODEL), F32),
            jax.ShapeDtypeStruct((N_EXPERTS, T), F32),
        ],
        compiler_params=pltpu.CompilerParams(
            dimension_semantics=("parallel",), vmem_limit_bytes=_vmem_limit(est)),
        name="out_proj",
    )(a, b, h, wa, wb, g, wr_hi, wr_lo)


def _select_kernel(aff_ref, idx_ref, gate_ref, *, cap):
    bits = pltpu.bitcast(aff_ref[0], jnp.int32)
    R = bits.shape[0]
    n_tok = R * LANES
    lane = lax.broadcasted_iota(jnp.int32, (R, LANES), 1)
    row = lax.broadcasted_iota(jnp.int32, (R, LANES), 0)

    def count(mask):
        ones_f = jnp.where(mask, 1.0, 0.0)
        return jnp.sum(jnp.sum(ones_f, axis=0, keepdims=True), axis=1, keepdims=True)

    thr = jnp.zeros((1, 1), jnp.int32)
    for bit in range(30, -1, -1):
        cand = thr | (1 << bit)
        thr = jnp.where(count(bits >= cand) >= cap, cand, thr)

    tri = (lax.broadcasted_iota(jnp.int32, (LANES, LANES), 0)
           <= lax.broadcasted_iota(jnp.int32, (LANES, LANES), 1)).astype(BF16)
    ones = jnp.ones((LANES, LANES), BF16)

    def shift_rows(x, s):
        return jnp.where(row >= s, pltpu.roll(x, s, 0), 0)

    def exclusive_rank(mask):
        m = mask.astype(BF16)
        incl = jnp.dot(m, tri, preferred_element_type=F32).astype(jnp.int32)
        tot = jnp.dot(m, ones, preferred_element_type=F32).astype(jnp.int32)
        acc = tot
        s = 1
        while s < R:
            acc = acc + shift_rows(acc, s)
            s *= 2
        return acc - tot + incl - mask.astype(jnp.int32)

    gt = bits > thr
    eq = bits == thr
    need = cap - count(gt)
    sel = gt | (eq & (exclusive_rank(eq) < need))
    tok = row * LANES + lane
    dist = jnp.where(sel, tok - exclusive_rank(sel), -1)

    def pull(x, a, fill):
        if a < LANES:
            near = pltpu.roll(x, LANES - a, 1)
            far = pltpu.roll(near, R - 1, 0) if R > 1 else near
            out = jnp.where(lane < LANES - a, near, far)
            valid = (row < R - 1) | (lane < LANES - a)
        else:
            s = a // LANES
            out = pltpu.roll(x, R - s, 0)
            valid = row < R - s
        return jnp.where(valid, out, fill)

    gbits = bits
    k = 0
    while (1 << k) < n_tok:
        a = 1 << k
        d_in = pull(dist, a, -1)
        t_in = pull(tok, a, 0)
        g_in = pull(gbits, a, 0)
        moves_in = (d_in >= 0) & (((d_in >> k) & 1) == 1)
        stays = (dist >= 0) & (((dist >> k) & 1) == 0)
        tok = jnp.where(moves_in, t_in, tok)
        gbits = jnp.where(moves_in, g_in, gbits)
        dist = jnp.where(moves_in, d_in, jnp.where(stays, dist, -1))
        k += 1

    rows_out = cap // LANES
    idx_ref[0] = tok[:rows_out]
    gate_ref[0] = pltpu.bitcast(gbits[:rows_out], F32)


def _select(aff_t, cap):
    T = aff_t.shape[1]
    R = T // LANES
    rows_out = cap // LANES
    aff3 = aff_t.reshape(N_EXPERTS, R, LANES)
    est = 40 * R * LANES * 4 + (2 << 20)
    idx, gate = pl.pallas_call(
        functools.partial(_select_kernel, cap=cap),
        grid=(N_EXPERTS,),
        in_specs=[pl.BlockSpec((1, R, LANES), lambda e: (e, 0, 0))],
        out_specs=[pl.BlockSpec((1, rows_out, LANES), lambda e: (e, 0, 0)),
                   pl.BlockSpec((1, rows_out, LANES), lambda e: (e, 0, 0))],
        out_shape=[jax.ShapeDtypeStruct((N_EXPERTS, rows_out, LANES), jnp.int32),
                   jax.ShapeDtypeStruct((N_EXPERTS, rows_out, LANES), F32)],
        compiler_params=pltpu.CompilerParams(
            dimension_semantics=("parallel",), vmem_limit_bytes=_vmem_limit(est)),
        name="select",
    )(aff3)
    return idx.reshape(N_EXPERTS, cap), gate.reshape(N_EXPERTS, cap)


FF_CHUNK = 512


def _moe_kernel(idx_ref, nidx_ref, gate_ref, g_ref, wg_ref, wu_ref, wd_ref, h_hbm, acc_in_hbm,
                acc_hbm, xbuf, abuf, obuf, xsem, asem, osem, *, tile):
    del acc_in_hbm
    e = pl.program_id(0)
    i = pl.program_id(1)
    nt = pl.num_programs(1)
    step = e * nt + i
    last = pl.num_programs(0) * nt - 1
    slot = step % 2

    def x_copy(ids, j, sl):
        return pltpu.make_async_copy(h_hbm.at[pl.ds(ids[0, 0, j], 1), :], xbuf.at[sl, pl.ds(j, 1), :], xsem.at[sl])

    def a_copy(j):
        return pltpu.make_async_copy(acc_hbm.at[pl.ds(idx_ref[0, 0, j], 1), :], abuf.at[pl.ds(j, 1), :], asem)

    def o_copy(j):
        return pltpu.make_async_copy(obuf.at[pl.ds(j, 1), :], acc_hbm.at[pl.ds(idx_ref[0, 0, j], 1), :], osem)

    def for_rows(fn):
        def body(j, c):
            fn(j)
            return c
        lax.fori_loop(0, tile, body, 0)

    @pl.when(step == 0)
    def _():
        for_rows(lambda j: x_copy(idx_ref, j, slot).start())

    for_rows(lambda j: x_copy(idx_ref, j, slot).wait())

    @pl.when(step < last)
    def _():
        for_rows(lambda j: x_copy(nidx_ref, j, 1 - slot).start())

    @pl.when(step > 0)
    def _():
        for_rows(lambda j: o_copy(j).wait())

    for_rows(lambda j: a_copy(j).start())

    xn = _rmsnorm(xbuf[slot], g_ref[...]).astype(BF16)
    y = jnp.zeros((tile, D_MODEL), F32)
    for c in range(EXPERT_FF // FF_CHUNK):
        cols = slice(c * FF_CHUNK, (c + 1) * FF_CHUNK)
        hg = jnp.dot(xn, wg_ref[0, :, cols], preferred_element_type=F32)
        hu = jnp.dot(xn, wu_ref[0, :, cols], preferred_element_type=F32)
        hid = (_silu(hg) * hu).astype(BF16)
        y = y + jnp.dot(hid, wd_ref[0, cols, :], preferred_element_type=F32)

    eye = lax.broadcasted_iota(jnp.int32, (tile, tile), 0) == lax.broadcasted_iota(jnp.int32, (tile, tile), 1)
    gcol = jnp.sum(jnp.where(eye, gate_ref[0], 0.0), axis=1, keepdims=True)

    for_rows(lambda j: a_copy(j).wait())
    obuf[...] = abuf[...] + y * gcol
    for_rows(lambda j: o_copy(j).start())

    @pl.when(step == last)
    def _():
        for_rows(lambda j: o_copy(j).wait())


def _moe(h, hacc, idx, gate, g, wg, wu, wd, tile):
    cap = idx.shape[1]
    nt = cap // tile
    idx3 = idx.reshape(N_EXPERTS * nt, 1, tile)
    gate3 = gate.reshape(N_EXPERTS * nt, 1, tile)
    n_steps = N_EXPERTS * nt
    smem_idx = lambda im: pl.BlockSpec((1, 1, tile), im, memory_space=pltpu.SMEM)
    wspec = lambda shape: pl.BlockSpec((1,) + shape, lambda e, i: (e, 0, 0))
    any_spec = pl.BlockSpec(memory_space=pl.ANY)
    est = 2 * 3 * D_MODEL * EXPERT_FF * 2 + 4 * tile * D_MODEL * 4 + 8 * tile * D_MODEL * 4 \
        + 6 * tile * FF_CHUNK * 4 + tile * tile * 4 * 2 + (2 << 20)
    return pl.pallas_call(
        functools.partial(_moe_kernel, tile=tile),
        grid=(N_EXPERTS, nt),
        in_specs=[
            smem_idx(lambda e, i: (e * nt + i, 0, 0)),
            smem_idx(lambda e, i: (jnp.minimum(e * nt + i + 1, n_steps - 1), 0, 0)),
            pl.BlockSpec((1, 1, tile), lambda e, i: (e * nt + i, 0, 0)),
            pl.BlockSpec((1, D_MODEL), lambda e, i: (0, 0)),
            wspec((D_MODEL, EXPERT_FF)),
            wspec((D_MODEL, EXPERT_FF)),
            wspec((EXPERT_FF, D_MODEL)),
            any_spec,
            any_spec,
        ],
        out_specs=any_spec,
        out_shape=jax.ShapeDtypeStruct(hacc.shape, F32),
        scratch_shapes=[
            pltpu.VMEM((2, tile, D_MODEL), F32),
            pltpu.VMEM((tile, D_MODEL), F32),
            pltpu.VMEM((tile, D_MODEL), F32),
            pltpu.SemaphoreType.DMA((2,)),
            pltpu.SemaphoreType.DMA(()),
            pltpu.SemaphoreType.DMA(()),
        ],
        input_output_aliases={8: 0},
        compiler_params=pltpu.CompilerParams(
            dimension_semantics=("arbitrary", "arbitrary"), vmem_limit_bytes=_vmem_limit(est)),
        name="moe",
    )(idx3, idx3, gate3, g, wg, wu, wd, h, hacc)


def _ple_kernel(h_ref, p_ref, g_ref, wg_ref, wp_ref, gf_ref, o_ref, *, final):
    h = h_ref[...]
    xn = _rmsnorm(h, g_ref[...]).astype(BF16)
    gate = _sigmoid(jnp.dot(xn, wg_ref[...], preferred_element_type=F32))
    proj = jnp.dot(p_ref[...].astype(BF16), wp_ref[...], preferred_element_type=F32)
    out = h + gate * proj
    if final:
        out = _rmsnorm(out, gf_ref[...])
    o_ref[...] = out


def _ple(h, p, g, wg, wp, gf, seq_len, final):
    T = h.shape[0]
    tm = _token_tile(seq_len)
    row = lambda i: (i, 0)
    const = lambda i: (0, 0)
    est = 2 * (2 * tm * D_MODEL * 4 + tm * PLE_DIM * 4) + 2 * (D_MODEL + PLE_DIM) * D_MODEL * 2 + 6 * tm * D_MODEL * 4
    return pl.pallas_call(
        functools.partial(_ple_kernel, final=final),
        grid=(T // tm,),
        in_specs=[
            pl.BlockSpec((tm, D_MODEL), row),
            pl.BlockSpec((tm, PLE_DIM), row),
            pl.BlockSpec((1, D_MODEL), const),
            pl.BlockSpec((D_MODEL, D_MODEL), const),
            pl.BlockSpec((PLE_DIM, D_MODEL), const),
            pl.BlockSpec((1, D_MODEL), const),
        ],
        out_specs=pl.BlockSpec((tm, D_MODEL), row),
        out_shape=jax.ShapeDtypeStruct((T, D_MODEL), F32),
        compiler_params=pltpu.CompilerParams(
            dimension_semantics=("parallel",), vmem_limit_bytes=_vmem_limit(est)),
        name="ple",
    )(h, p, g, wg, wp, gf)


def _qk_column_order():
    half = HEAD_DIM // 2
    order = []
    for pair in range(RET_HEADS // 2):
        for part in range(2):
            for hh in range(2):
                base = (2 * pair + hh) * HEAD_DIM + part * half
                order.extend(range(base, base + half))
    return jnp.asarray(order, jnp.int32)


def _rope_tables(seq_len):
    half = HEAD_DIM // 2
    inv = 1.0 / (ROPE_BASE ** (jnp.arange(half, dtype=F32) / half))
    ang = jnp.arange(seq_len, dtype=F32)[:, None] * inv[None, :]
    cos, sin = jnp.cos(ang), jnp.sin(ang)
    return (jnp.concatenate([cos, cos, cos, cos], axis=1),
            jnp.concatenate([-sin, -sin, sin, sin], axis=1))


def _prep_layer(i, w_in, w_out, w_router, w_exp_gate, w_exp_up, w_exp_down, w_ple_gate, w_ple_proj):
    order = _qk_column_order()
    w = w_in[i]
    wq = w[:, SPLITS[0]:SPLITS[1]][:, order]
    wk = w[:, SPLITS[1]:SPLITS[2]][:, order]
    w_in_p = jnp.concatenate([w[:, :SPLITS[0]], wq, wk, w[:, SPLITS[2]:]], axis=1).astype(BF16)
    wr_t = w_router[i].T
    wr_hi = wr_t.astype(BF16)
    wr_lo = (wr_t - wr_hi.astype(F32)).astype(BF16)
    wo = w_out[i].astype(BF16)
    return dict(
        w_in=w_in_p, wo_a=wo[:CONV_CH], wo_b=wo[CONV_CH:], wr_hi=wr_hi, wr_lo=wr_lo,
        wg=w_exp_gate[i].astype(BF16), wu=w_exp_up[i].astype(BF16), wd=w_exp_down[i].astype(BF16),
        w_ple_gate=w_ple_gate[i].astype(BF16), w_ple_proj=w_ple_proj[i].astype(BF16))


def _moe_tile(cap, want=512):
    t = want
    while cap % t:
        t //= 2
    return t


def _trunk(x, p, layers, vecs, norm_final):
    batch, seq_len, _ = x.shape
    T = batch * seq_len
    cap = CAP_FACTOR * T // N_EXPERTS
    cos_t, sin_t = _rope_tables(seq_len)
    h = x.reshape(T, D_MODEL)
    depth = len(layers)
    for i, (lw, lv) in enumerate(zip(layers, vecs)):
        u, q, k, v, og = _in_proj(h, lv["norm_mix"], lw["w_in"], cos_t, sin_t, seq_len)
        a_out = _conv(u, lv["conv_w"], lv["conv_b"], lv["conv_ln_g"], lv["conv_ln_b"], batch, seq_len)
        b_out = _retention(q, k, v, og, lv["lg_f"], lv["lg_b"], lv["gn_g"], batch, seq_len)
        h_mid, h_acc, aff_t = _out_proj(a_out, b_out, h, lw["wo_a"], lw["wo_b"], lv["norm_ffn"],
                                        lw["wr_hi"], lw["wr_lo"], seq_len)
        idx, gate = _select(aff_t, cap)
        h_acc = _moe(h_mid, h_acc, idx, gate, lv["norm_ffn"], lw["wg"], lw["wu"], lw["wd"], _moe_tile(cap))
        h = _ple(h_acc, p[i].reshape(T, PLE_DIM), lv["norm_ple"], lw["w_ple_gate"], lw["w_ple_proj"],
                 norm_final, seq_len, final=(i == depth - 1))
    return h.reshape(batch, seq_len, D_MODEL)


def kernel(x_prompt, x_sample, p_prompt, p_sample, norm_mix, w_in, conv_w, conv_b, conv_ln_g, conv_ln_b,
           ret_log_gamma_fwd, ret_log_gamma_bwd, ret_gn_g, w_out, norm_ffn, w_router, w_exp_gate, w_exp_up,
           w_exp_down, norm_ple, w_ple_gate, w_ple_proj, norm_final):
    depth = w_in.shape[0]
    layers = [_prep_layer(i, w_in, w_out, w_router, w_exp_gate, w_exp_up, w_exp_down, w_ple_gate, w_ple_proj)
              for i in range(depth)]
    vecs = [dict(
        norm_mix=norm_mix[i].reshape(1, D_MODEL), conv_w=conv_w[i], conv_b=conv_b[i].reshape(1, CONV_CH),
        conv_ln_g=conv_ln_g[i].reshape(1, CONV_CH), conv_ln_b=conv_ln_b[i].reshape(1, CONV_CH),
        lg_f=ret_log_gamma_fwd[i], lg_b=ret_log_gamma_bwd[i], gn_g=ret_gn_g[i].reshape(1, RET_WIDTH),
        norm_ffn=norm_ffn[i].reshape(1, D_MODEL), norm_ple=norm_ple[i].reshape(1, D_MODEL))
        for i in range(depth)]
    gf = norm_final.reshape(1, D_MODEL)
    y_prompt = _trunk(x_prompt, p_prompt, layers, vecs, gf)
    y_sample = _trunk(x_sample, p_sample, layers, vecs, gf)
    return (y_prompt, y_sample)
```

```python
import functools
import math

import jax
import jax.numpy as jnp
from jax import lax
from jax.experimental import pallas as pl
from jax.experimental.pallas import tpu as pltpu

D_MODEL = 1024
CONV_CH = 512
RET_HEADS = 8
HEAD_DIM = 64
RET_WIDTH = RET_HEADS * HEAD_DIM
CONV_WIDTH = 31
CONV_PAD = CONV_WIDTH // 2
CHUNK = 128
ROPE_BASE = 10000.0
N_EXPERTS = 16
CAP_FACTOR = 2
EXPERT_FF = 2 * D_MODEL
PLE_DIM = 256
EPS = 1e-6
SPLITS = (2 * CONV_CH, 2 * CONV_CH + RET_WIDTH, 2 * CONV_CH + 2 * RET_WIDTH,
          2 * CONV_CH + 3 * RET_WIDTH, 2 * CONV_CH + 4 * RET_WIDTH)

LANES = 128
SUBLANES = 8
V7X_VMEM_BYTES = 64 * 1024 * 1024
V7X_VMEM_USABLE = 56 * 1024 * 1024
ROW_TILES = D_MODEL // LANES

BF16 = jnp.bfloat16
F32 = jnp.float32
NT_DIMS = (((1,), (1,)), ((), ()))
TN_DIMS = (((0,), (0,)), ((), ()))


def _vmem_limit(estimate_bytes):
    return int(min(V7X_VMEM_USABLE, max(16 * 1024 * 1024, estimate_bytes)))


def _token_tile(seq_len, want=1024):
    tm = want
    while seq_len % tm:
        tm //= 2
    return tm


def _rmsnorm(x, g):
    y = x * lax.rsqrt(jnp.mean(x * x, axis=-1, keepdims=True) + EPS)
    return y * g


def _silu(x):
    return x * (1.0 / (1.0 + jnp.exp(-x)))


def _sigmoid(x):
    return 1.0 / (1.0 + jnp.exp(-x))


def _in_proj_kernel(h_ref, g_ref, w_ref, cq_ref, sq_ref, u_ref, q_ref, k_ref, v_ref, og_ref):
    xn = _rmsnorm(h_ref[...], g_ref[...]).astype(BF16)

    def seg(lo, hi):
        return jnp.dot(xn, w_ref[:, lo:hi], preferred_element_type=F32)

    u_ref[...] = seg(0, SPLITS[0]).astype(BF16)
    cos = cq_ref[...]
    sin = sq_ref[...]
    scale = HEAD_DIM ** -0.5
    for lo, ref, mul in ((SPLITS[0], q_ref, 1.0), (SPLITS[1], k_ref, scale)):
        for t in range(RET_WIDTH // LANES):
            x = seg(lo + t * LANES, lo + (t + 1) * LANES)
            r = x * cos + pltpu.roll(x, LANES // 2, 1) * sin
            if mul != 1.0:
                r = r * mul
            ref[:, t * LANES:(t + 1) * LANES] = r.astype(BF16)
    v_ref[...] = seg(SPLITS[2], SPLITS[3]).astype(BF16)
    og_ref[...] = seg(SPLITS[3], SPLITS[4]).astype(BF16)


def _in_proj(h, g, w, cos_t, sin_t, seq_len):
    T = h.shape[0]
    tm = _token_tile(seq_len)
    nl = seq_len // tm
    row = lambda i: (i, 0)
    est = 2 * (tm * D_MODEL * 4 + tm * SPLITS[4] * 2 + 2 * tm * LANES * 4) + 2 * D_MODEL * SPLITS[4] * 2 \
        + 6 * tm * D_MODEL * 4
    return pl.pallas_call(
        _in_proj_kernel,
        grid=(T // tm,),
        in_specs=[
            pl.BlockSpec((tm, D_MODEL), row),
            pl.BlockSpec((1, D_MODEL), lambda i: (0, 0)),
            pl.BlockSpec((D_MODEL, SPLITS[4]), lambda i: (0, 0)),
            pl.BlockSpec((tm, LANES), lambda i: (i % nl, 0)),
            pl.BlockSpec((tm, LANES), lambda i: (i % nl, 0)),
        ],
        out_specs=[
            pl.BlockSpec((tm, 2 * CONV_CH), row),
            pl.BlockSpec((tm, RET_WIDTH), row),
            pl.BlockSpec((tm, RET_WIDTH), row),
            pl.BlockSpec((tm, RET_WIDTH), row),
            pl.BlockSpec((tm, RET_WIDTH), row),
        ],
        out_shape=[
            jax.ShapeDtypeStruct((T, 2 * CONV_CH), BF16),
            jax.ShapeDtypeStruct((T, RET_WIDTH), BF16),
            jax.ShapeDtypeStruct((T, RET_WIDTH), BF16),
            jax.ShapeDtypeStruct((T, RET_WIDTH), BF16),
            jax.ShapeDtypeStruct((T, RET_WIDTH), BF16),
        ],
        compiler_params=pltpu.CompilerParams(
            dimension_semantics=("parallel",), vmem_limit_bytes=_vmem_limit(est)),
        name="in_proj",
    )(h, g, w, cos_t, sin_t)


CONV_ROWS = 128
CONV_HALO = 16
CONV_WIN = CONV_ROWS + 2 * CONV_HALO


def _conv_kernel(u_ref, w_ref, b_ref, lg_ref, lb_ref, o_ref, hp_ref, cv_ref, *, seq_len):
    n_chunks = seq_len // CONV_ROWS
    zeros = jnp.zeros((CONV_HALO, CONV_CH), F32)
    hp_ref[0:CONV_HALO, :] = zeros
    hp_ref[CONV_HALO + seq_len:CONV_HALO + seq_len + CONV_HALO, :] = zeros

    def glu(ci, c):
        r0 = pl.multiple_of(ci * CONV_ROWS, CONV_ROWS)
        rows = u_ref[0, pl.ds(r0, CONV_ROWS), :].astype(F32)
        hp_ref[pl.ds(CONV_HALO + r0, CONV_ROWS), :] = rows[:, :CONV_CH] * _sigmoid(rows[:, CONV_CH:])
        return c

    lax.fori_loop(0, n_chunks, glu, 0)

    def conv(ci, c):
        r0 = pl.multiple_of(ci * CONV_ROWS, CONV_ROWS)
        for t in range(CONV_CH // LANES):
            cols = slice(t * LANES, (t + 1) * LANES)
            win = hp_ref[pl.ds(r0, CONV_WIN), cols]
            acc = jnp.zeros((CONV_ROWS, LANES), F32)
            for phase in range(SUBLANES):
                offs = [o for o in range(CONV_HALO - CONV_PAD, CONV_HALO - CONV_PAD + CONV_WIDTH)
                        if o % SUBLANES == phase]
                if not offs:
                    continue
                shifted = win if phase == 0 else pltpu.roll(win, CONV_WIN - phase, 0)
                for o in offs:
                    j = o - (CONV_HALO - CONV_PAD)
                    base = o - phase
                    acc = acc + shifted[base:base + CONV_ROWS] * w_ref[j:j + 1, cols]
            cv_ref[:, cols] = acc + b_ref[:, cols]
        y = cv_ref[...]
        mu = jnp.mean(y, axis=-1, keepdims=True)
        yc = y - mu
        var = jnp.mean(yc * yc, axis=-1, keepdims=True)
        z = yc * lax.rsqrt(var + EPS) * lg_ref[...] + lb_ref[...]
        o_ref[0, pl.ds(r0, CONV_ROWS), :] = _silu(z).astype(BF16)
        return c

    lax.fori_loop(0, n_chunks, conv, 0)


def _conv(u, w, b, lg, lb, batch, seq_len):
    u3 = u.reshape(batch, seq_len, 2 * CONV_CH)
    est = 2 * (seq_len * 2 * CONV_CH * 2 + seq_len * CONV_CH * 2) + (seq_len + 2 * CONV_HALO) * CONV_CH * 4 \
        + 8 * CONV_ROWS * CONV_CH * 4 + (1 << 20)
    const = lambda i: (0, 0)
    out = pl.pallas_call(
        functools.partial(_conv_kernel, seq_len=seq_len),
        grid=(batch,),
        in_specs=[
            pl.BlockSpec((1, seq_len, 2 * CONV_CH), lambda i: (i, 0, 0)),
            pl.BlockSpec((CONV_WIDTH, CONV_CH), const),
            pl.BlockSpec((1, CONV_CH), const),
            pl.BlockSpec((1, CONV_CH), const),
            pl.BlockSpec((1, CONV_CH), const),
        ],
        out_specs=pl.BlockSpec((1, seq_len, CONV_CH), lambda i: (i, 0, 0)),
        out_shape=jax.ShapeDtypeStruct((batch, seq_len, CONV_CH), BF16),
        scratch_shapes=[
            pltpu.VMEM((seq_len + 2 * CONV_HALO, CONV_CH), F32),
            pltpu.VMEM((CONV_ROWS, CONV_CH), F32),
        ],
        compiler_params=pltpu.CompilerParams(
            dimension_semantics=("parallel",), vmem_limit_bytes=_vmem_limit(est)),
        name="conv",
    )(u3, w, b, lg, lb)
    return out.reshape(batch * seq_len, CONV_CH)


def _dot2(x, m):
    hi = x.astype(BF16)
    lo = (x - hi.astype(F32)).astype(BF16)
    return jnp.dot(hi, m, preferred_element_type=F32) + jnp.dot(lo, m, preferred_element_type=F32)


RET_UNROLL = 8
RET_NORM_ROWS = 256


def _retention_kernel(lgf_ref, lgb_ref, q_ref, k_ref, v_ref, og_ref, gn_ref, o_ref, sf_ref, sb_ref, p_ref,
                      acc_ref, *, seq_len):
    nc = seq_len // CHUNK
    pair = pl.program_id(1)
    lgf = (lgf_ref[2 * pair], lgf_ref[2 * pair + 1])
    lgb = (lgb_ref[2 * pair], lgb_ref[2 * pair + 1])

    lane = lax.broadcasted_iota(jnp.int32, (CHUNK, LANES), 1)
    row = lax.broadcasted_iota(jnp.int32, (CHUNK, LANES), 0)
    rowf = row.astype(F32)
    qk_head1 = ((lane // (HEAD_DIM // 2)) % 2) == 1
    v_head1 = lane >= HEAD_DIM
    krow_head1 = ((row // (HEAD_DIM // 2)) % 2) == 1
    same_head = krow_head1 == v_head1

    def per_lane(pairvals, head1):
        return jnp.where(head1, pairvals[1], pairvals[0])

    lgf_qk = per_lane(lgf, qk_head1)
    lgb_qk = per_lane(lgb, qk_head1)
    lgf_v = per_lane(lgf, v_head1)
    lgb_v = per_lane(lgb, v_head1)
    zeta_f = jnp.exp(lgf_qk * (CHUNK - 1.0 - rowf))
    zeta_b = jnp.exp(lgb_qk * rowf)
    xi_f = jnp.exp(lgf_v * (rowf + 1.0))
    xi_b = jnp.exp(lgb_v * (CHUNK - rowf))
    dg_f = jnp.exp(per_lane(lgf, krow_head1) * float(CHUNK))
    dg_b = jnp.exp(per_lane(lgb, krow_head1) * float(CHUNK))
    diff = (row - lane).astype(F32)
    decay = []
    for hh in range(2):
        fwd = jnp.exp(lgf[hh] * jnp.maximum(diff, 0.0))
        bwd = jnp.exp(lgb[hh] * jnp.maximum(-diff, 0.0))
        decay.append(jnp.where(diff >= 0.0, fwd, bwd))
    decay2 = jnp.concatenate(decay, axis=0)
    group_mean = jnp.where((row >= HEAD_DIM) == v_head1, 1.0 / HEAD_DIM, 0.0).astype(BF16)

    def chunk_rows(c):
        return pl.ds(pl.multiple_of(c * CHUNK, CHUNK), CHUNK)

    def kv_update(kc, vc, zeta):
        kz = (kc.astype(F32) * zeta).astype(BF16)
        upd = lax.dot_general(kz, vc, TN_DIMS, preferred_element_type=F32)
        return jnp.where(same_head, upd, 0.0)

    def state_body(i, carry):
        sf, sb = carry
        cb = nc - 1 - i
        sf_ref[i] = sf.astype(BF16)
        sb_ref[cb] = sb.astype(BF16)
        rf = chunk_rows(i)
        rb = chunk_rows(cb)
        sf = sf * dg_f + kv_update(k_ref[rf, :], v_ref[rf, :], zeta_f)
        sb = sb * dg_b + kv_update(k_ref[rb, :], v_ref[rb, :], zeta_b)
        return sf, sb

    zero_state = jnp.zeros((LANES, LANES), F32)
    lax.fori_loop(0, nc, state_body, (zero_state, zero_state), unroll=RET_UNROLL)

    def score_body(c, carry):
        rows = chunk_rows(c)
        qc = q_ref[rows, :]
        zero = jnp.zeros_like(qc)
        q2 = jnp.concatenate([jnp.where(qk_head1, zero, qc), jnp.where(qk_head1, qc, zero)], axis=0)
        s = lax.dot_general(q2, k_ref[rows, :], NT_DIMS, preferred_element_type=F32)
        p = (s * decay2).astype(BF16)
        p_ref[c] = jnp.concatenate([p[:CHUNK], p[CHUNK:]], axis=1)
        return carry

    lax.fori_loop(0, nc, score_body, 0, unroll=RET_UNROLL)

    def value_body(c, carry):
        rows = chunk_rows(c)
        qc = q_ref[rows, :]
        vc = v_ref[rows, :]
        zero = jnp.zeros_like(vc)
        v2 = jnp.concatenate([jnp.where(v_head1, zero, vc), jnp.where(v_head1, vc, zero)], axis=0)
        o = jnp.dot(p_ref[c], v2, preferred_element_type=F32)
        o = o + xi_f * jnp.dot(qc, sf_ref[c], preferred_element_type=F32)
        o = o + xi_b * jnp.dot(qc, sb_ref[c], preferred_element_type=F32)
        acc_ref[rows, :] = o
        return carry

    lax.fori_loop(0, nc, value_body, 0, unroll=RET_UNROLL)

    def mean_body(b, carry):
        rows = pl.ds(pl.multiple_of(b * RET_NORM_ROWS, RET_NORM_ROWS), RET_NORM_ROWS)
        o = acc_ref[rows, :]
        acc_ref[rows, :] = o - _dot2(o, group_mean)
        return carry

    def norm_body(b, carry):
        rows = pl.ds(pl.multiple_of(b * RET_NORM_ROWS, RET_NORM_ROWS), RET_NORM_ROWS)
        oc = acc_ref[rows, :]
        var = _dot2(oc * oc, group_mean)
        on = oc * lax.rsqrt(var + EPS) * gn_ref[...]
        o_ref[rows, :] = (_silu(og_ref[rows, :].astype(F32)) * on).astype(BF16)
        return carry

    lax.fori_loop(0, seq_len // RET_NORM_ROWS, mean_body, 0, unroll=2)
    lax.fori_loop(0, seq_len // RET_NORM_ROWS, norm_body, 0, unroll=2)


def _retention(q, k, v, og, lgf, lgb, gn, batch, seq_len):
    T = batch * seq_len
    blk = pl.BlockSpec((seq_len, LANES), lambda b, p: (b, p))
    smem = pl.BlockSpec(memory_space=pltpu.SMEM)
    est = 2 * 5 * seq_len * LANES * 2 + 4 * (seq_len // CHUNK) * LANES * LANES * 2 + seq_len * LANES * 4 + (8 << 20)
    return pl.pallas_call(
        functools.partial(_retention_kernel, seq_len=seq_len),
        grid=(batch, RET_WIDTH // LANES),
        in_specs=[smem, smem, blk, blk, blk, blk, pl.BlockSpec((1, LANES), lambda b, p: (0, p))],
        out_specs=blk,
        out_shape=jax.ShapeDtypeStruct((T, RET_WIDTH), BF16),
        scratch_shapes=[pltpu.VMEM((seq_len // CHUNK, LANES, LANES), BF16),
                        pltpu.VMEM((seq_len // CHUNK, LANES, LANES), BF16),
                        pltpu.VMEM((seq_len // CHUNK, CHUNK, 2 * CHUNK), BF16),
                        pltpu.VMEM((seq_len, LANES), F32)],
        compiler_params=pltpu.CompilerParams(
            dimension_semantics=("parallel", "parallel"), vmem_limit_bytes=_vmem_limit(est)),
        name="retention",
    )(lgf, lgb, q, k, v, og, gn)


def _out_proj_kernel(a_ref, b_ref, h_ref, wa_ref, wb_ref, g_ref, wrh_ref, wrl_ref, hn_ref, hacc_ref, aff_ref):
    y = jnp.dot(a_ref[...], wa_ref[...], preferred_element_type=F32)
    y = y + jnp.dot(b_ref[...], wb_ref[...], preferred_element_type=F32)
    hn = h_ref[...] + y
    tm = hn.shape[0]
    for s in range(ROW_TILES):
        part = hn[:, s * LANES:(s + 1) * LANES]
        hn_ref[pl.ds(s, tm, stride=ROW_TILES), :] = part
        hacc_ref[pl.ds(s, tm, stride=ROW_TILES), :] = part
    xn = _rmsnorm(hn, g_ref[...])
    x_hi = xn.astype(BF16)
    x_lo = (xn - x_hi.astype(F32)).astype(BF16)
    wr_hi = wrh_ref[...]
    logits = lax.dot_general(wr_hi, x_hi, NT_DIMS, preferred_element_type=F32)
    logits = logits + lax.dot_general(wr_hi, x_lo, NT_DIMS, preferred_element_type=F32)
    logits = logits + lax.dot_general(wrl_ref[...], x_hi, NT_DIMS, preferred_element_type=F32)
    e = jnp.exp(logits - jnp.max(logits, axis=0, keepdims=True))
    aff_ref[...] = e / jnp.sum(e, axis=0, keepdims=True)


def _out_proj(a, b, h, wa, wb, g, wr_hi, wr_lo, seq_len):
    T = h.shape[0]
    tm = _token_tile(seq_len)
    row = lambda i: (i, 0)
    const = lambda i: (0, 0)
    est = 2 * (2 * tm * CONV_CH * 2 + 3 * tm * D_MODEL * 4 + N_EXPERTS * tm * 4) + 2 * D_MODEL * D_MODEL * 2 \
        + 6 * tm * D_MODEL * 4
    return pl.pallas_call(
        _out_proj_kernel,
        grid=(T // tm,),
        in_specs=[
            pl.BlockSpec((tm, CONV_CH), row),
            pl.BlockSpec((tm, RET_WIDTH), row),
            pl.BlockSpec((tm, D_MODEL), row),
            pl.BlockSpec((CONV_CH, D_MODEL), const),
            pl.BlockSpec((RET_WIDTH, D_MODEL), const),
            pl.BlockSpec((1, D_MODEL), const),
            pl.BlockSpec((N_EXPERTS, D_MODEL), const),
            pl.BlockSpec((N_EXPERTS, D_MODEL), const),
        ],
        out_specs=[
            pl.BlockSpec((tm * ROW_TILES, LANES), row),
            pl.BlockSpec((tm * ROW_TILES, LANES), row),
            pl.BlockSpec((N_EXPERTS, tm), lambda i: (0, i)),
        ],
        out_shape=[
            jax.ShapeDtypeStruct((T * ROW_TILES, LANES), F32),
            jax.ShapeDtypeStruct((T * ROW_TILES, LANES), F32),
            jax.ShapeDtypeStruct((N_EXPERTS, T), F32),
        ],
        compiler_params=pltpu.CompilerParams(
            dimension_semantics=("parallel",), vmem_limit_bytes=_vmem_limit(est)),
        name="out_proj",
    )(a, b, h, wa, wb, g, wr_hi, wr_lo)


def _select_kernel(aff_ref, idx_ref, gate_ref, *, cap):
    bits = pltpu.bitcast(aff_ref[0], jnp.int32)
    R = bits.shape[0]
    n_tok = R * LANES
    lane = lax.broadcasted_iota(jnp.int32, (R, LANES), 1)
    row = lax.broadcasted_iota(jnp.int32, (R, LANES), 0)

    def count(mask):
        ones_f = jnp.where(mask, 1.0, 0.0)
        return jnp.sum(jnp.sum(ones_f, axis=0, keepdims=True), axis=1, keepdims=True)

    thr = jnp.zeros((1, 1), jnp.int32)
    for bit in range(30, -1, -1):
        cand = thr | (1 << bit)
        thr = jnp.where(count(bits >= cand) >= cap, cand, thr)

    tri = (lax.broadcasted_iota(jnp.int32, (LANES, LANES), 0)
           <= lax.broadcasted_iota(jnp.int32, (LANES, LANES), 1)).astype(BF16)
    ones = jnp.ones((LANES, LANES), BF16)

    def shift_rows(x, s):
        return jnp.where(row >= s, pltpu.roll(x, s, 0), 0)

    def exclusive_rank(mask):
        m = mask.astype(BF16)
        incl = jnp.dot(m, tri, preferred_element_type=F32).astype(jnp.int32)
        tot = jnp.dot(m, ones, preferred_element_type=F32).astype(jnp.int32)
        acc = tot
        s = 1
        while s < R:
            acc = acc + shift_rows(acc, s)
            s *= 2
        return acc - tot + incl - mask.astype(jnp.int32)

    gt = bits > thr
    eq = bits == thr
    need = cap - count(gt)
    sel = gt | (eq & (exclusive_rank(eq) < need))
    tok = row * LANES + lane
    dist = jnp.where(sel, tok - exclusive_rank(sel), -1)

    def pull(x, a, fill):
        if a < LANES:
            near = pltpu.roll(x, LANES - a, 1)
            far = pltpu.roll(near, R - 1, 0) if R > 1 else near
            out = jnp.where(lane < LANES - a, near, far)
            valid = (row < R - 1) | (lane < LANES - a)
        else:
            s = a // LANES
            out = pltpu.roll(x, R - s, 0)
            valid = row < R - s
        return jnp.where(valid, out, fill)

    gbits = bits
    k = 0
    while (1 << k) < n_tok:
        a = 1 << k
        d_in = pull(dist, a, -1)
        t_in = pull(tok, a, 0)
        g_in = pull(gbits, a, 0)
        moves_in = (d_in >= 0) & (((d_in >> k) & 1) == 1)
        stays = (dist >= 0) & (((dist >> k) & 1) == 0)
        tok = jnp.where(moves_in, t_in, tok)
        gbits = jnp.where(moves_in, g_in, gbits)
        dist = jnp.where(moves_in, d_in, jnp.where(stays, dist, -1))
        k += 1

    rows_out = cap // LANES
    idx_ref[0] = tok[:rows_out]
    gate_ref[0] = pltpu.bitcast(gbits[:rows_out], F32)


def _select(aff_t, cap):
    T = aff_t.shape[1]
    R = T // LANES
    rows_out = cap // LANES
    aff3 = aff_t.reshape(N_EXPERTS, R, LANES)
    est = 40 * R * LANES * 4 + (2 << 20)
    idx, gate = pl.pallas_call(
        functools.partial(_select_kernel, cap=cap),
        grid=(N_EXPERTS,),
        in_specs=[pl.BlockSpec((1, R, LANES), lambda e: (e, 0, 0))],
        out_specs=[pl.BlockSpec((1, rows_out, LANES), lambda e: (e, 0, 0)),
                   pl.BlockSpec((1, rows_out, LANES), lambda e: (e, 0, 0))],
        out_shape=[jax.ShapeDtypeStruct((N_EXPERTS, rows_out, LANES), jnp.int32),
                   jax.ShapeDtypeStruct((N_EXPERTS, rows_out, LANES), F32)],
        compiler_params=pltpu.CompilerParams(
            dimension_semantics=("parallel",), vmem_limit_bytes=_vmem_limit(est)),
        name="select",
    )(aff3)
    return idx.reshape(N_EXPERTS, cap), gate.reshape(N_EXPERTS, cap)


FF_CHUNK = 512


MOE_PITCH = ROW_TILES + 1
MOE_PIECES = 6


def _moe_kernel(pidx_ref, idx_ref, nidx_ref, gate_ref, g_ref, wg_ref, wu_ref, wd_ref, hx_hbm, acc_in_hbm,
                acc_hbm, xbuf, abuf, obuf, xn_ref, xsem, asem, osem, *, tile):
    del acc_in_hbm
    nt = pl.num_programs(1)
    step = pl.program_id(0) * nt + pl.program_id(1)
    last = pl.num_programs(0) * nt - 1
    slot = step % 2
    other = 1 - slot

    def token_rows(ref, t):
        return ref.at[pl.ds(pl.multiple_of(t * ROW_TILES, ROW_TILES), ROW_TILES), :]

    def buf_rows(j):
        return pl.ds(j * MOE_PITCH, ROW_TILES)

    def x_copy(ids, j, sl):
        return pltpu.make_async_copy(token_rows(hx_hbm, ids[0, 0, j]), xbuf.at[sl, buf_rows(j), :], xsem.at[sl])

    def a_copy(j):
        return pltpu.make_async_copy(token_rows(acc_hbm, idx_ref[0, 0, j]), abuf.at[buf_rows(j), :], asem)

    def o_copy(ids, j, sl):
        return pltpu.make_async_copy(obuf.at[sl, buf_rows(j), :], token_rows(acc_hbm, ids[0, 0, j]), osem.at[sl])

    def seed_copy(j):
        return pltpu.make_async_copy(token_rows(acc_hbm, idx_ref[0, 0, j]), obuf.at[other, buf_rows(j), :],
                                     osem.at[other])

    def for_rows(fn):
        def body(j, c):
            fn(j)
            return c
        lax.fori_loop(0, tile, body, 0)

    @pl.when(step == 0)
    def _():
        for_rows(lambda j: x_copy(idx_ref, j, slot).start())
        for_rows(lambda j: seed_copy(j).start())
        for_rows(lambda j: seed_copy(j).wait())

    for j in range(tile):
        x_copy(idx_ref, j, slot).wait()
    x = jnp.concatenate([xbuf[slot, pl.ds(s, tile, stride=MOE_PITCH), :] for s in range(ROW_TILES)], axis=1)
    xn_ref[...] = _rmsnorm(x, g_ref[...]).astype(BF16)

    issue_plan = {
        0: [functools.partial(lambda j: o_copy(pidx_ref, j, other).start(), j) for j in range(tile)],
        1: [functools.partial(lambda j: x_copy(nidx_ref, j, other).start(), j) for j in range(tile)],
        2: [functools.partial(lambda j: a_copy(j).start(), j) for j in range(tile)],
    }
    half_ff = FF_CHUNK // 2
    half_d = D_MODEL // 2
    y0 = jnp.zeros((tile, half_d), F32)
    y1 = jnp.zeros((tile, half_d), F32)
    for c in range(EXPERT_FF // FF_CHUNK):
        batch = issue_plan.get(c, [])
        per = -(-len(batch) // MOE_PIECES)

        def issue(k):
            for fn in batch[k * per:(k + 1) * per]:
                fn()

        lo = c * FF_CHUNK
        mid = lo + half_ff
        hi = lo + FF_CHUNK
        issue(0)
        hg0 = jnp.dot(xn_ref[...], wg_ref[0, :, lo:mid], preferred_element_type=F32)
        issue(1)
        hg1 = jnp.dot(xn_ref[...], wg_ref[0, :, mid:hi], preferred_element_type=F32)
        issue(2)
        hu0 = jnp.dot(xn_ref[...], wu_ref[0, :, lo:mid], preferred_element_type=F32)
        issue(3)
        hu1 = jnp.dot(xn_ref[...], wu_ref[0, :, mid:hi], preferred_element_type=F32)
        hid = jnp.concatenate([_silu(hg0) * hu0, _silu(hg1) * hu1], axis=1).astype(BF16)
        issue(4)
        y0 = y0 + jnp.dot(hid, wd_ref[0, lo:hi, :half_d], preferred_element_type=F32)
        issue(5)
        y1 = y1 + jnp.dot(hid, wd_ref[0, lo:hi, half_d:], preferred_element_type=F32)
        if c == 1:
            for j in range(tile):
                o_copy(pidx_ref, j, other).wait()

    eye = lax.broadcasted_iota(jnp.int32, (tile, tile), 0) == lax.broadcasted_iota(jnp.int32, (tile, tile), 1)
    gcol = jnp.sum(jnp.where(eye, gate_ref[0], 0.0), axis=1, keepdims=True)

    for j in range(tile):
        a_copy(j).wait()
    for s in range(ROW_TILES):
        ys = (y0 if s < ROW_TILES // 2 else y1)[:, (s % (ROW_TILES // 2)) * LANES:(s % (ROW_TILES // 2) + 1) * LANES]
        rows = pl.ds(s, tile, stride=MOE_PITCH)
        obuf[slot, rows, :] = abuf[rows, :] + ys * gcol

    @pl.when(step == last)
    def _():
        for_rows(lambda j: o_copy(idx_ref, j, slot).start())
        for_rows(lambda j: o_copy(idx_ref, j, slot).wait())
        for_rows(lambda j: x_copy(nidx_ref, j, other).wait())


def _moe(h, hacc, idx, gate, g, wg, wu, wd, tile):
    cap = idx.shape[1]
    nt = cap // tile
    idx3 = idx.reshape(N_EXPERTS * nt, 1, tile)
    gate3 = gate.reshape(N_EXPERTS * nt, 1, tile)
    n_steps = N_EXPERTS * nt
    smem_idx = lambda im: pl.BlockSpec((1, 1, tile), im, memory_space=pltpu.SMEM)
    wspec = lambda shape: pl.BlockSpec((1,) + shape, lambda e, i: (e, 0, 0))
    any_spec = pl.BlockSpec(memory_space=pl.ANY)
    buf_bytes = tile * MOE_PITCH * LANES * 4
    est = 2 * 3 * D_MODEL * EXPERT_FF * 2 + 5 * buf_bytes + tile * D_MODEL * 2 + 8 * tile * D_MODEL * 4 \
        + 6 * tile * FF_CHUNK * 4 + tile * tile * 4 * 2 + (2 << 20)
    return pl.pallas_call(
        functools.partial(_moe_kernel, tile=tile),
        grid=(N_EXPERTS, nt),
        in_specs=[
            smem_idx(lambda e, i: (jnp.maximum(e * nt + i - 1, 0), 0, 0)),
            smem_idx(lambda e, i: (e * nt + i, 0, 0)),
            smem_idx(lambda e, i: (jnp.minimum(e * nt + i + 1, n_steps - 1), 0, 0)),
            pl.BlockSpec((1, 1, tile), lambda e, i: (e * nt + i, 0, 0)),
            pl.BlockSpec((1, D_MODEL), lambda e, i: (0, 0)),
            wspec((D_MODEL, EXPERT_FF)),
            wspec((D_MODEL, EXPERT_FF)),
            wspec((EXPERT_FF, D_MODEL)),
            any_spec,
            any_spec,
        ],
        out_specs=any_spec,
        out_shape=jax.ShapeDtypeStruct(hacc.shape, F32),
        scratch_shapes=[
            pltpu.VMEM((2, tile * MOE_PITCH, LANES), F32),
            pltpu.VMEM((tile * MOE_PITCH, LANES), F32),
            pltpu.VMEM((2, tile * MOE_PITCH, LANES), F32),
            pltpu.VMEM((tile, D_MODEL), BF16),
            pltpu.SemaphoreType.DMA((2,)),
            pltpu.SemaphoreType.DMA(()),
            pltpu.SemaphoreType.DMA((2,)),
        ],
        input_output_aliases={9: 0},
        compiler_params=pltpu.CompilerParams(
            dimension_semantics=("arbitrary", "arbitrary"), vmem_limit_bytes=_vmem_limit(est)),
        name="moe",
    )(idx3, idx3, idx3, gate3, g, wg, wu, wd, h, hacc)


def _ple_kernel(h_ref, p_ref, g_ref, wg_ref, wp_ref, gf_ref, o_ref, *, final):
    tm = o_ref.shape[0]
    h = jnp.concatenate([h_ref[pl.ds(s, tm, stride=ROW_TILES), :] for s in range(ROW_TILES)], axis=1)
    xn = _rmsnorm(h, g_ref[...]).astype(BF16)
    gate = _sigmoid(jnp.dot(xn, wg_ref[...], preferred_element_type=F32))
    proj = jnp.dot(p_ref[...].astype(BF16), wp_ref[...], preferred_element_type=F32)
    out = h + gate * proj
    if final:
        out = _rmsnorm(out, gf_ref[...])
    o_ref[...] = out


def _ple(h, p, g, wg, wp, gf, seq_len, final):
    T = h.shape[0] // ROW_TILES
    tm = _token_tile(seq_len)
    row = lambda i: (i, 0)
    const = lambda i: (0, 0)
    est = 2 * (2 * tm * D_MODEL * 4 + tm * PLE_DIM * 4) + 2 * (D_MODEL + PLE_DIM) * D_MODEL * 2 + 6 * tm * D_MODEL * 4
    return pl.pallas_call(
        functools.partial(_ple_kernel, final=final),
        grid=(T // tm,),
        in_specs=[
            pl.BlockSpec((tm * ROW_TILES, LANES), row),
            pl.BlockSpec((tm, PLE_DIM), row),
            pl.BlockSpec((1, D_MODEL), const),
            pl.BlockSpec((D_MODEL, D_MODEL), const),
            pl.BlockSpec((PLE_DIM, D_MODEL), const),
            pl.BlockSpec((1, D_MODEL), const),
        ],
        out_specs=pl.BlockSpec((tm, D_MODEL), row),
        out_shape=jax.ShapeDtypeStruct((T, D_MODEL), F32),
        compiler_params=pltpu.CompilerParams(
            dimension_semantics=("parallel",), vmem_limit_bytes=_vmem_limit(est)),
        name="ple",
    )(h, p, g, wg, wp, gf)


def _qk_column_order():
    half = HEAD_DIM // 2
    order = []
    for pair in range(RET_HEADS // 2):
        for part in range(2):
            for hh in range(2):
                base = (2 * pair + hh) * HEAD_DIM + part * half
                order.extend(range(base, base + half))
    return jnp.asarray(order, jnp.int32)


def _rope_tables(seq_len):
    half = HEAD_DIM // 2
    inv = 1.0 / (ROPE_BASE ** (jnp.arange(half, dtype=F32) / half))
    ang = jnp.arange(seq_len, dtype=F32)[:, None] * inv[None, :]
    cos, sin = jnp.cos(ang), jnp.sin(ang)
    return (jnp.concatenate([cos, cos, cos, cos], axis=1),
            jnp.concatenate([-sin, -sin, sin, sin], axis=1))


def _prep_layer(i, w_in, w_out, w_router, w_exp_gate, w_exp_up, w_exp_down, w_ple_gate, w_ple_proj):
    order = _qk_column_order()
    w = w_in[i]
    wq = w[:, SPLITS[0]:SPLITS[1]][:, order]
    wk = w[:, SPLITS[1]:SPLITS[2]][:, order]
    w_in_p = jnp.concatenate([w[:, :SPLITS[0]], wq, wk, w[:, SPLITS[2]:]], axis=1).astype(BF16)
    wr_t = w_router[i].T
    wr_hi = wr_t.astype(BF16)
    wr_lo = (wr_t - wr_hi.astype(F32)).astype(BF16)
    wo = w_out[i].astype(BF16)
    return dict(
        w_in=w_in_p, wo_a=wo[:CONV_CH], wo_b=wo[CONV_CH:], wr_hi=wr_hi, wr_lo=wr_lo,
        wg=w_exp_gate[i].astype(BF16), wu=w_exp_up[i].astype(BF16), wd=w_exp_down[i].astype(BF16),
        w_ple_gate=w_ple_gate[i].astype(BF16), w_ple_proj=w_ple_proj[i].astype(BF16))


def _moe_tile(cap, want=512):
    t = want
    while cap % t:
        t //= 2
    return t


def _trunk(x, p, layers, vecs, norm_final):
    batch, seq_len, _ = x.shape
    T = batch * seq_len
    cap = CAP_FACTOR * T // N_EXPERTS
    cos_t, sin_t = _rope_tables(seq_len)
    h = x.reshape(T, D_MODEL)
    depth = len(layers)
    for i, (lw, lv) in enumerate(zip(layers, vecs)):
        u, q, k, v, og = _in_proj(h, lv["norm_mix"], lw["w_in"], cos_t, sin_t, seq_len)
        a_out = _conv(u, lv["conv_w"], lv["conv_b"], lv["conv_ln_g"], lv["conv_ln_b"], batch, seq_len)
        b_out = _retention(q, k, v, og, lv["lg_f"], lv["lg_b"], lv["gn_g"], batch, seq_len)
        h_mid, h_acc, aff_t = _out_proj(a_out, b_out, h, lw["wo_a"], lw["wo_b"], lv["norm_ffn"],
                                        lw["wr_hi"], lw["wr_lo"], seq_len)
        idx, gate = _select(aff_t, cap)
        h_acc = _moe(h_mid, h_acc, idx, gate, lv["norm_ffn"], lw["wg"], lw["wu"], lw["wd"], _moe_tile(cap))
        h = _ple(h_acc, p[i].reshape(T, PLE_DIM), lv["norm_ple"], lw["w_ple_gate"], lw["w_ple_proj"],
                 norm_final, seq_len, final=(i == depth - 1))
    return h.reshape(batch, seq_len, D_MODEL)


def kernel(x_prompt, x_sample, p_prompt, p_sample, norm_mix, w_in, conv_w, conv_b, conv_ln_g, conv_ln_b,
           ret_log_gamma_fwd, ret_log_gamma_bwd, ret_gn_g, w_out, norm_ffn, w_router, w_exp_gate, w_exp_up,
           w_exp_down, norm_ple, w_ple_gate, w_ple_proj, norm_final):
    depth = w_in.shape[0]
    layers = [_prep_layer(i, w_in, w_out, w_router, w_exp_gate, w_exp_up, w_exp_down, w_ple_gate, w_ple_proj)
              for i in range(depth)]
    vecs = [dict(
        norm_mix=norm_mix[i].reshape(1, D_MODEL), conv_w=conv_w[i], conv_b=conv_b[i].reshape(1, CONV_CH),
        conv_ln_g=conv_ln_g[i].reshape(1, CONV_CH), conv_ln_b=conv_ln_b[i].reshape(1, CONV_CH),
        lg_f=ret_log_gamma_fwd[i], lg_b=ret_log_gamma_bwd[i], gn_g=ret_gn_g[i].reshape(1, RET_WIDTH),
        norm_ffn=norm_ffn[i].reshape(1, D_MODEL), norm_ple=norm_ple[i].reshape(1, D_MODEL))
        for i in range(depth)]
    gf = norm_final.reshape(1, D_MODEL)
    y_prompt = _trunk(x_prompt, p_prompt, layers, vecs, gf)
    y_sample = _trunk(x_sample, p_sample, layers, vecs, gf)
    return (y_prompt, y_sample)
```

```python
import functools
import math

import jax
import jax.numpy as jnp
from jax import lax
from jax.experimental import pallas as pl
from jax.experimental.pallas import tpu as pltpu

D_MODEL = 1024
CONV_CH = 512
RET_HEADS = 8
HEAD_DIM = 64
RET_WIDTH = RET_HEADS * HEAD_DIM
CONV_WIDTH = 31
CONV_PAD = CONV_WIDTH // 2
CHUNK = 128
ROPE_BASE = 10000.0
N_EXPERTS = 16
CAP_FACTOR = 2
EXPERT_FF = 2 * D_MODEL
PLE_DIM = 256
EPS = 1e-6
SPLITS = (2 * CONV_CH, 2 * CONV_CH + RET_WIDTH, 2 * CONV_CH + 2 * RET_WIDTH,
          2 * CONV_CH + 3 * RET_WIDTH, 2 * CONV_CH + 4 * RET_WIDTH)

LANES = 128
SUBLANES = 8
V7X_VMEM_BYTES = 64 * 1024 * 1024
V7X_VMEM_USABLE = 56 * 1024 * 1024
ROW_TILES = D_MODEL // LANES
PACK_TILES = ROW_TILES // 2

BF16 = jnp.bfloat16
F32 = jnp.float32
NT_DIMS = (((1,), (1,)), ((), ()))
TN_DIMS = (((0,), (0,)), ((), ()))


def _vmem_limit(estimate_bytes):
    return int(min(V7X_VMEM_USABLE, max(16 * 1024 * 1024, estimate_bytes)))


def _token_tile(seq_len, want=1024):
    tm = want
    while seq_len % tm:
        tm //= 2
    return tm


def _rmsnorm(x, g):
    y = x * lax.rsqrt(jnp.mean(x * x, axis=-1, keepdims=True) + EPS)
    return y * g


def _silu(x):
    return x * (1.0 / (1.0 + jnp.exp(-x)))


def _sigmoid(x):
    return 1.0 / (1.0 + jnp.exp(-x))


def _in_proj_kernel(h_ref, g_ref, w_ref, cq_ref, sq_ref, u_ref, q_ref, k_ref, v_ref, og_ref):
    xn = _rmsnorm(h_ref[...], g_ref[...]).astype(BF16)

    def seg(lo, hi):
        return jnp.dot(xn, w_ref[:, lo:hi], preferred_element_type=F32)

    u_ref[...] = seg(0, SPLITS[0]).astype(BF16)
    cos = cq_ref[...]
    sin = sq_ref[...]
    scale = HEAD_DIM ** -0.5
    for lo, ref, mul in ((SPLITS[0], q_ref, 1.0), (SPLITS[1], k_ref, scale)):
        full = seg(lo, lo + RET_WIDTH)
        for t in range(RET_WIDTH // LANES):
            x = full[:, t * LANES:(t + 1) * LANES]
            r = x * cos + pltpu.roll(x, LANES // 2, 1) * sin
            if mul != 1.0:
                r = r * mul
            ref[:, t * LANES:(t + 1) * LANES] = r.astype(BF16)
    v_ref[...] = seg(SPLITS[2], SPLITS[3]).astype(BF16)
    og_ref[...] = seg(SPLITS[3], SPLITS[4]).astype(BF16)


def _in_proj(h, g, w, cos_t, sin_t, seq_len):
    T = h.shape[0]
    tm = _token_tile(seq_len)
    nl = seq_len // tm
    row = lambda i: (i, 0)
    est = 2 * (tm * D_MODEL * 4 + tm * SPLITS[4] * 2 + 2 * tm * LANES * 4) + 2 * D_MODEL * SPLITS[4] * 2 \
        + 6 * tm * D_MODEL * 4
    return pl.pallas_call(
        _in_proj_kernel,
        grid=(T // tm,),
        in_specs=[
            pl.BlockSpec((tm, D_MODEL), row),
            pl.BlockSpec((1, D_MODEL), lambda i: (0, 0)),
            pl.BlockSpec((D_MODEL, SPLITS[4]), lambda i: (0, 0)),
            pl.BlockSpec((tm, LANES), lambda i: (i % nl, 0)),
            pl.BlockSpec((tm, LANES), lambda i: (i % nl, 0)),
        ],
        out_specs=[
            pl.BlockSpec((tm, 2 * CONV_CH), row),
            pl.BlockSpec((tm, RET_WIDTH), row),
            pl.BlockSpec((tm, RET_WIDTH), row),
            pl.BlockSpec((tm, RET_WIDTH), row),
            pl.BlockSpec((tm, RET_WIDTH), row),
        ],
        out_shape=[
            jax.ShapeDtypeStruct((T, 2 * CONV_CH), BF16),
            jax.ShapeDtypeStruct((T, RET_WIDTH), BF16),
            jax.ShapeDtypeStruct((T, RET_WIDTH), BF16),
            jax.ShapeDtypeStruct((T, RET_WIDTH), BF16),
            jax.ShapeDtypeStruct((T, RET_WIDTH), BF16),
        ],
        compiler_params=pltpu.CompilerParams(
            dimension_semantics=("parallel",), vmem_limit_bytes=_vmem_limit(est)),
        name="in_proj",
    )(h, g, w, cos_t, sin_t)


CONV_ROWS = 128
CONV_HALO = 16
CONV_WIN = CONV_ROWS + 2 * CONV_HALO


def _conv_kernel(u_ref, w_ref, b_ref, lg_ref, lb_ref, o_ref, hp_ref, cv_ref, *, seq_len):
    n_chunks = seq_len // CONV_ROWS
    zeros = jnp.zeros((CONV_HALO, CONV_CH), F32)
    hp_ref[0:CONV_HALO, :] = zeros
    hp_ref[CONV_HALO + seq_len:CONV_HALO + seq_len + CONV_HALO, :] = zeros

    def glu(ci, c):
        r0 = pl.multiple_of(ci * CONV_ROWS, CONV_ROWS)
        rows = u_ref[0, pl.ds(r0, CONV_ROWS), :].astype(F32)
        hp_ref[pl.ds(CONV_HALO + r0, CONV_ROWS), :] = rows[:, :CONV_CH] * _sigmoid(rows[:, CONV_CH:])
        return c

    lax.fori_loop(0, n_chunks, glu, 0)

    def conv(ci, c):
        r0 = pl.multiple_of(ci * CONV_ROWS, CONV_ROWS)
        for t in range(CONV_CH // LANES):
            cols = slice(t * LANES, (t + 1) * LANES)
            win = hp_ref[pl.ds(r0, CONV_WIN), cols]
            acc = jnp.zeros((CONV_ROWS, LANES), F32)
            for phase in range(SUBLANES):
                offs = [o for o in range(CONV_HALO - CONV_PAD, CONV_HALO - CONV_PAD + CONV_WIDTH)
                        if o % SUBLANES == phase]
                if not offs:
                    continue
                shifted = win if phase == 0 else pltpu.roll(win, CONV_WIN - phase, 0)
                for o in offs:
                    j = o - (CONV_HALO - CONV_PAD)
                    base = o - phase
                    acc = acc + shifted[base:base + CONV_ROWS] * w_ref[j:j + 1, cols]
            cv_ref[:, cols] = acc + b_ref[:, cols]
        y = cv_ref[...]
        mu = jnp.mean(y, axis=-1, keepdims=True)
        yc = y - mu
        var = jnp.mean(yc * yc, axis=-1, keepdims=True)
        z = yc * lax.rsqrt(var + EPS) * lg_ref[...] + lb_ref[...]
        o_ref[0, pl.ds(r0, CONV_ROWS), :] = _silu(z).astype(BF16)
        return c

    lax.fori_loop(0, n_chunks, conv, 0)


def _conv(u, w, b, lg, lb, batch, seq_len):
    u3 = u.reshape(batch, seq_len, 2 * CONV_CH)
    est = 2 * (seq_len * 2 * CONV_CH * 2 + seq_len * CONV_CH * 2) + (seq_len + 2 * CONV_HALO) * CONV_CH * 4 \
        + 8 * CONV_ROWS * CONV_CH * 4 + (1 << 20)
    const = lambda i: (0, 0)
    out = pl.pallas_call(
        functools.partial(_conv_kernel, seq_len=seq_len),
        grid=(batch,),
        in_specs=[
            pl.BlockSpec((1, seq_len, 2 * CONV_CH), lambda i: (i, 0, 0)),
            pl.BlockSpec((CONV_WIDTH, CONV_CH), const),
            pl.BlockSpec((1, CONV_CH), const),
            pl.BlockSpec((1, CONV_CH), const),
            pl.BlockSpec((1, CONV_CH), const),
        ],
        out_specs=pl.BlockSpec((1, seq_len, CONV_CH), lambda i: (i, 0, 0)),
        out_shape=jax.ShapeDtypeStruct((batch, seq_len, CONV_CH), BF16),
        scratch_shapes=[
            pltpu.VMEM((seq_len + 2 * CONV_HALO, CONV_CH), F32),
            pltpu.VMEM((CONV_ROWS, CONV_CH), F32),
        ],
        compiler_params=pltpu.CompilerParams(
            dimension_semantics=("parallel",), vmem_limit_bytes=_vmem_limit(est)),
        name="conv",
    )(u3, w, b, lg, lb)
    return out.reshape(batch * seq_len, CONV_CH)


def _dot2(x, m):
    hi = x.astype(BF16)
    lo = (x - hi.astype(F32)).astype(BF16)
    return jnp.dot(hi, m, preferred_element_type=F32) + jnp.dot(lo, m, preferred_element_type=F32)


RET_UNROLL = 8
RET_NORM_ROWS = 256


def _retention_kernel(lgf_ref, lgb_ref, q_ref, k_ref, v_ref, og_ref, gn_ref, o_ref, sf_ref, sb_ref, p_ref,
                      acc_ref, *, seq_len):
    nc = seq_len // CHUNK
    pair = pl.program_id(1)
    lgf = (lgf_ref[2 * pair], lgf_ref[2 * pair + 1])
    lgb = (lgb_ref[2 * pair], lgb_ref[2 * pair + 1])

    lane = lax.broadcasted_iota(jnp.int32, (CHUNK, LANES), 1)
    row = lax.broadcasted_iota(jnp.int32, (CHUNK, LANES), 0)
    rowf = row.astype(F32)
    qk_head1 = ((lane // (HEAD_DIM // 2)) % 2) == 1
    v_head1 = lane >= HEAD_DIM
    krow_head1 = ((row // (HEAD_DIM // 2)) % 2) == 1
    same_head = krow_head1 == v_head1

    def per_lane(pairvals, head1):
        return jnp.where(head1, pairvals[1], pairvals[0])

    lgf_qk = per_lane(lgf, qk_head1)
    lgb_qk = per_lane(lgb, qk_head1)
    lgf_v = per_lane(lgf, v_head1)
    lgb_v = per_lane(lgb, v_head1)
    zeta_f = jnp.exp(lgf_qk * (CHUNK - 1.0 - rowf))
    zeta_b = jnp.exp(lgb_qk * rowf)
    xi_f = jnp.exp(lgf_v * (rowf + 1.0))
    xi_b = jnp.exp(lgb_v * (CHUNK - rowf))
    dg_f = jnp.exp(per_lane(lgf, krow_head1) * float(CHUNK))
    dg_b = jnp.exp(per_lane(lgb, krow_head1) * float(CHUNK))
    diff = (row - lane).astype(F32)
    decay = []
    for hh in range(2):
        fwd = jnp.exp(lgf[hh] * jnp.maximum(diff, 0.0))
        bwd = jnp.exp(lgb[hh] * jnp.maximum(-diff, 0.0))
        decay.append(jnp.where(diff >= 0.0, fwd, bwd))
    decay2 = jnp.concatenate(decay, axis=0)
    group_mean = jnp.where((row >= HEAD_DIM) == v_head1, 1.0 / HEAD_DIM, 0.0).astype(BF16)

    def chunk_rows(c):
        return pl.ds(pl.multiple_of(c * CHUNK, CHUNK), CHUNK)

    def kv_update(kc, vc, zeta):
        kz = (kc.astype(F32) * zeta).astype(BF16)
        upd = lax.dot_general(kz, vc, TN_DIMS, preferred_element_type=F32)
        return jnp.where(same_head, upd, 0.0)

    def state_body(i, carry):
        sf, sb = carry
        cb = nc - 1 - i
        sf_ref[i] = sf.astype(BF16)
        sb_ref[cb] = sb.astype(BF16)
        rf = chunk_rows(i)
        rb = chunk_rows(cb)
        sf = sf * dg_f + kv_update(k_ref[rf, :], v_ref[rf, :], zeta_f)
        sb = sb * dg_b + kv_update(k_ref[rb, :], v_ref[rb, :], zeta_b)
        return sf, sb

    zero_state = jnp.zeros((LANES, LANES), F32)
    lax.fori_loop(0, nc, state_body, (zero_state, zero_state), unroll=RET_UNROLL)

    def score_body(c, carry):
        rows = chunk_rows(c)
        qc = q_ref[rows, :]
        zero = jnp.zeros_like(qc)
        q2 = jnp.concatenate([jnp.where(qk_head1, zero, qc), jnp.where(qk_head1, qc, zero)], axis=0)
        s = lax.dot_general(q2, k_ref[rows, :], NT_DIMS, preferred_element_type=F32)
        p = (s * decay2).astype(BF16)
        p_ref[c] = jnp.concatenate([p[:CHUNK], p[CHUNK:]], axis=1)
        return carry

    lax.fori_loop(0, nc, score_body, 0, unroll=RET_UNROLL)

    def value_body(c, carry):
        rows = chunk_rows(c)
        qc = q_ref[rows, :]
        vc = v_ref[rows, :]
        zero = jnp.zeros_like(vc)
        v2 = jnp.concatenate([jnp.where(v_head1, zero, vc), jnp.where(v_head1, vc, zero)], axis=0)
        o = jnp.dot(p_ref[c], v2, preferred_element_type=F32)
        o = o + xi_f * jnp.dot(qc, sf_ref[c], preferred_element_type=F32)
        o = o + xi_b * jnp.dot(qc, sb_ref[c], preferred_element_type=F32)
        acc_ref[rows, :] = o
        return carry

    lax.fori_loop(0, nc, value_body, 0, unroll=RET_UNROLL)

    def mean_body(b, carry):
        rows = pl.ds(pl.multiple_of(b * RET_NORM_ROWS, RET_NORM_ROWS), RET_NORM_ROWS)
        o = acc_ref[rows, :]
        acc_ref[rows, :] = o - _dot2(o, group_mean)
        return carry

    def norm_body(b, carry):
        rows = pl.ds(pl.multiple_of(b * RET_NORM_ROWS, RET_NORM_ROWS), RET_NORM_ROWS)
        oc = acc_ref[rows, :]
        var = _dot2(oc * oc, group_mean)
        on = oc * lax.rsqrt(var + EPS) * gn_ref[...]
        o_ref[rows, :] = (_silu(og_ref[rows, :].astype(F32)) * on).astype(BF16)
        return carry

    lax.fori_loop(0, seq_len // RET_NORM_ROWS, mean_body, 0, unroll=2)
    lax.fori_loop(0, seq_len // RET_NORM_ROWS, norm_body, 0, unroll=2)


def _retention(q, k, v, og, lgf, lgb, gn, batch, seq_len):
    T = batch * seq_len
    blk = pl.BlockSpec((seq_len, LANES), lambda b, p: (b, p))
    smem = pl.BlockSpec(memory_space=pltpu.SMEM)
    est = 2 * 5 * seq_len * LANES * 2 + 4 * (seq_len // CHUNK) * LANES * LANES * 2 + seq_len * LANES * 4 + (8 << 20)
    return pl.pallas_call(
        functools.partial(_retention_kernel, seq_len=seq_len),
        grid=(batch, RET_WIDTH // LANES),
        in_specs=[smem, smem, blk, blk, blk, blk, pl.BlockSpec((1, LANES), lambda b, p: (0, p))],
        out_specs=blk,
        out_shape=jax.ShapeDtypeStruct((T, RET_WIDTH), BF16),
        scratch_shapes=[pltpu.VMEM((seq_len // CHUNK, LANES, LANES), BF16),
                        pltpu.VMEM((seq_len // CHUNK, LANES, LANES), BF16),
                        pltpu.VMEM((seq_len // CHUNK, CHUNK, 2 * CHUNK), BF16),
                        pltpu.VMEM((seq_len, LANES), F32)],
        compiler_params=pltpu.CompilerParams(
            dimension_semantics=("parallel", "parallel"), vmem_limit_bytes=_vmem_limit(est)),
        name="retention",
    )(lgf, lgb, q, k, v, og, gn)


def _out_proj_kernel(a_ref, b_ref, h_ref, wa_ref, wb_ref, g_ref, wrh_ref, wrl_ref, xp_ref, hacc_ref, aff_ref):
    y = jnp.dot(a_ref[...], wa_ref[...], preferred_element_type=F32)
    y = y + jnp.dot(b_ref[...], wb_ref[...], preferred_element_type=F32)
    hn = h_ref[...] + y
    tm = hn.shape[0]
    for s in range(ROW_TILES):
        hacc_ref[pl.ds(s, tm, stride=ROW_TILES), :] = hn[:, s * LANES:(s + 1) * LANES]
    xn = _rmsnorm(hn, g_ref[...])
    x_hi = xn.astype(BF16)
    bits = pltpu.bitcast(x_hi.astype(F32), jnp.uint32)
    packed = (bits[:, D_MODEL // 2:] & jnp.uint32(0xFFFF0000)) | (bits[:, :D_MODEL // 2] >> 16)
    for s in range(PACK_TILES):
        xp_ref[pl.ds(s, tm, stride=PACK_TILES), :] = packed[:, s * LANES:(s + 1) * LANES]
    x_lo = (xn - x_hi.astype(F32)).astype(BF16)
    wr_hi = wrh_ref[...]
    logits = lax.dot_general(wr_hi, x_hi, NT_DIMS, preferred_element_type=F32)
    logits = logits + lax.dot_general(wr_hi, x_lo, NT_DIMS, preferred_element_type=F32)
    logits = logits + lax.dot_general(wrl_ref[...], x_hi, NT_DIMS, preferred_element_type=F32)
    e = jnp.exp(logits - jnp.max(logits, axis=0, keepdims=True))
    aff_ref[...] = e / jnp.sum(e, axis=0, keepdims=True)


def _out_proj(a, b, h, wa, wb, g, wr_hi, wr_lo, seq_len):
    T = h.shape[0]
    tm = _token_tile(seq_len)
    row = lambda i: (i, 0)
    const = lambda i: (0, 0)
    est = 2 * (2 * tm * CONV_CH * 2 + 3 * tm * D_MODEL * 4 + N_EXPERTS * tm * 4) + 2 * D_MODEL * D_MODEL * 2 \
        + 6 * tm * D_MODEL * 4
    return pl.pallas_call(
        _out_proj_kernel,
        grid=(T // tm,),
        in_specs=[
            pl.BlockSpec((tm, CONV_CH), row),
            pl.BlockSpec((tm, RET_WIDTH), row),
            pl.BlockSpec((tm, D_MODEL), row),
            pl.BlockSpec((CONV_CH, D_MODEL), const),
            pl.BlockSpec((RET_WIDTH, D_MODEL), const),
            pl.BlockSpec((1, D_MODEL), const),
            pl.BlockSpec((N_EXPERTS, D_MODEL), const),
            pl.BlockSpec((N_EXPERTS, D_MODEL), const),
        ],
        out_specs=[
            pl.BlockSpec((tm * PACK_TILES, LANES), row),
            pl.BlockSpec((tm * ROW_TILES, LANES), row),
            pl.BlockSpec((N_EXPERTS, tm), lambda i: (0, i)),
        ],
        out_shape=[
            jax.ShapeDtypeStruct((T * PACK_TILES, LANES), jnp.uint32),
            jax.ShapeDtypeStruct((T * ROW_TILES, LANES), F32),
            jax.ShapeDtypeStruct((N_EXPERTS, T), F32),
        ],
        compiler_params=pltpu.CompilerParams(
            dimension_semantics=("parallel",), vmem_limit_bytes=_vmem_limit(est)),
        name="out_proj",
    )(a, b, h, wa, wb, g, wr_hi, wr_lo)


def _select_kernel(aff_ref, idx_ref, gate_ref, *, cap):
    bits = pltpu.bitcast(aff_ref[0], jnp.int32)
    R = bits.shape[0]
    n_tok = R * LANES
    lane = lax.broadcasted_iota(jnp.int32, (R, LANES), 1)
    row = lax.broadcasted_iota(jnp.int32, (R, LANES), 0)

    def count(mask):
        ones_f = jnp.where(mask, 1.0, 0.0)
        return jnp.sum(jnp.sum(ones_f, axis=0, keepdims=True), axis=1, keepdims=True)

    thr = jnp.zeros((1, 1), jnp.int32)
    for bit in range(30, -1, -1):
        cand = thr | (1 << bit)
        thr = jnp.where(count(bits >= cand) >= cap, cand, thr)

    tri = (lax.broadcasted_iota(jnp.int32, (LANES, LANES), 0)
           <= lax.broadcasted_iota(jnp.int32, (LANES, LANES), 1)).astype(BF16)
    ones = jnp.ones((LANES, LANES), BF16)

    def shift_rows(x, s):
        return jnp.where(row >= s, pltpu.roll(x, s, 0), 0)

    def exclusive_rank(mask):
        m = mask.astype(BF16)
        incl = jnp.dot(m, tri, preferred_element_type=F32).astype(jnp.int32)
        tot = jnp.dot(m, ones, preferred_element_type=F32).astype(jnp.int32)
        acc = tot
        s = 1
        while s < R:
            acc = acc + shift_rows(acc, s)
            s *= 2
        return acc - tot + incl - mask.astype(jnp.int32)

    gt = bits > thr
    eq = bits == thr
    need = cap - count(gt)
    sel = gt | (eq & (exclusive_rank(eq) < need))
    tok = row * LANES + lane
    dist = jnp.where(sel, tok - exclusive_rank(sel), -1)

    def pull(x, a, fill):
        if a < LANES:
            near = pltpu.roll(x, LANES - a, 1)
            far = pltpu.roll(near, R - 1, 0) if R > 1 else near
            out = jnp.where(lane < LANES - a, near, far)
            valid = (row < R - 1) | (lane < LANES - a)
        else:
            s = a // LANES
            out = pltpu.roll(x, R - s, 0)
            valid = row < R - s
        return jnp.where(valid, out, fill)

    gbits = bits
    k = 0
    while (1 << k) < n_tok:
        a = 1 << k
        d_in = pull(dist, a, -1)
        t_in = pull(tok, a, 0)
        g_in = pull(gbits, a, 0)
        moves_in = (d_in >= 0) & (((d_in >> k) & 1) == 1)
        stays = (dist >= 0) & (((dist >> k) & 1) == 0)
        tok = jnp.where(moves_in, t_in, tok)
        gbits = jnp.where(moves_in, g_in, gbits)
        dist = jnp.where(moves_in, d_in, jnp.where(stays, dist, -1))
        k += 1

    rows_out = cap // LANES
    idx_ref[0] = tok[:rows_out]
    gate_ref[0] = pltpu.bitcast(gbits[:rows_out], F32)


def _select(aff_t, cap):
    T = aff_t.shape[1]
    R = T // LANES
    rows_out = cap // LANES
    aff3 = aff_t.reshape(N_EXPERTS, R, LANES)
    est = 40 * R * LANES * 4 + (2 << 20)
    idx, gate = pl.pallas_call(
        functools.partial(_select_kernel, cap=cap),
        grid=(N_EXPERTS,),
        in_specs=[pl.BlockSpec((1, R, LANES), lambda e: (e, 0, 0))],
        out_specs=[pl.BlockSpec((1, rows_out, LANES), lambda e: (e, 0, 0)),
                   pl.BlockSpec((1, rows_out, LANES), lambda e: (e, 0, 0))],
        out_shape=[jax.ShapeDtypeStruct((N_EXPERTS, rows_out, LANES), jnp.int32),
                   jax.ShapeDtypeStruct((N_EXPERTS, rows_out, LANES), F32)],
        compiler_params=pltpu.CompilerParams(
            dimension_semantics=("parallel",), vmem_limit_bytes=_vmem_limit(est)),
        name="select",
    )(aff3)
    return idx.reshape(N_EXPERTS, cap), gate.reshape(N_EXPERTS, cap)


FF_CHUNK = 512


MOE_PITCH = ROW_TILES + 1
MOE_XPITCH = PACK_TILES + 1
MOE_PIECES = 6
GATE_ROWS = 512


def _moe_kernel(pidx_ref, idx_ref, nidx_ref, gate_ref, wg_ref, wu_ref, wd_ref, xp_hbm, acc_in_hbm,
                acc_hbm, xbuf, abuf, obuf, xn_ref, xsem, asem, osem, *, tile):
    del acc_in_hbm
    nt = pl.num_programs(1)
    step = pl.program_id(0) * nt + pl.program_id(1)
    last = pl.num_programs(0) * nt - 1
    slot = step % 2
    other = 1 - slot

    def token_rows(ref, t):
        return ref.at[pl.ds(pl.multiple_of(t * ROW_TILES, ROW_TILES), ROW_TILES), :]

    def buf_rows(j):
        return pl.ds(j * MOE_PITCH, ROW_TILES)

    def x_copy(ids, j, sl):
        src = xp_hbm.at[pl.ds(pl.multiple_of(ids[0, 0, j] * PACK_TILES, PACK_TILES), PACK_TILES), :]
        return pltpu.make_async_copy(src, xbuf.at[sl, pl.ds(j * MOE_XPITCH, PACK_TILES), :], xsem.at[sl])

    def a_copy(j):
        return pltpu.make_async_copy(token_rows(acc_hbm, idx_ref[0, 0, j]), abuf.at[buf_rows(j), :], asem)

    def o_copy(ids, j, sl):
        return pltpu.make_async_copy(obuf.at[sl, buf_rows(j), :], token_rows(acc_hbm, ids[0, 0, j]), osem.at[sl])

    def seed_copy(j):
        return pltpu.make_async_copy(token_rows(acc_hbm, idx_ref[0, 0, j]), obuf.at[other, buf_rows(j), :],
                                     osem.at[other])

    def for_rows(fn):
        def body(j, c):
            fn(j)
            return c
        lax.fori_loop(0, tile, body, 0)

    @pl.when(step == 0)
    def _():
        for_rows(lambda j: x_copy(idx_ref, j, slot).start())
        for_rows(lambda j: seed_copy(j).start())
        for_rows(lambda j: seed_copy(j).wait())

    for j in range(tile):
        x_copy(idx_ref, j, slot).wait()
    for s in range(PACK_TILES):
        w = xbuf[slot, pl.ds(s, tile, stride=MOE_XPITCH), :]
        lo = pltpu.bitcast(w << 16, F32)
        hi = pltpu.bitcast(w & jnp.uint32(0xFFFF0000), F32)
        xn_ref[:, s * LANES:(s + 1) * LANES] = lo.astype(BF16)
        xn_ref[:, D_MODEL // 2 + s * LANES:D_MODEL // 2 + (s + 1) * LANES] = hi.astype(BF16)

    issue_plan = {
        0: [functools.partial(lambda j: o_copy(pidx_ref, j, other).start(priority=j % 2), j) for j in range(tile)],
        1: [functools.partial(lambda j: x_copy(nidx_ref, j, other).start(priority=j % 2), j) for j in range(tile)],
        2: [functools.partial(lambda j: a_copy(j).start(priority=j % 2), j) for j in range(tile)],
    }
    half_ff = FF_CHUNK // 2
    half_d = D_MODEL // 2
    n_ff = EXPERT_FF // FF_CHUNK
    y0 = jnp.zeros((tile, half_d), F32)
    y1 = jnp.zeros((tile, half_d), F32)
    for c in range(n_ff):
        batch = issue_plan.get(c, [])
        per = -(-len(batch) // MOE_PIECES)

        def issue(k):
            for fn in batch[k * per:(k + 1) * per]:
                fn()

        lo = c * FF_CHUNK
        mid = lo + half_ff
        hi = lo + FF_CHUNK
        issue(0)
        hg0 = jnp.dot(xn_ref[...], wg_ref[0, :, lo:mid], preferred_element_type=F32)
        issue(1)
        hg1 = jnp.dot(xn_ref[...], wg_ref[0, :, mid:hi], preferred_element_type=F32)
        issue(2)
        hu0 = jnp.dot(xn_ref[...], wu_ref[0, :, lo:mid], preferred_element_type=F32)
        issue(3)
        hu1 = jnp.dot(xn_ref[...], wu_ref[0, :, mid:hi], preferred_element_type=F32)
        hid = jnp.concatenate([_silu(hg0) * hu0, _silu(hg1) * hu1], axis=1).astype(BF16)
        issue(4)
        y0 = y0 + jnp.dot(hid, wd_ref[0, lo:hi, :half_d], preferred_element_type=F32)
        issue(5)
        y1 = y1 + jnp.dot(hid, wd_ref[0, lo:hi, half_d:], preferred_element_type=F32)
        if c == 1:
            for j in range(tile):
                o_copy(pidx_ref, j, other).wait()
    gparts = []
    for r0 in range(0, tile, GATE_ROWS):
        rr = min(GATE_ROWS, tile - r0)
        eye = (lax.broadcasted_iota(jnp.int32, (rr, tile), 0) + r0) == lax.broadcasted_iota(jnp.int32, (rr, tile), 1)
        gparts.append(jnp.sum(jnp.where(eye, gate_ref[0], 0.0), axis=1, keepdims=True))
    gcol = gparts[0] if len(gparts) == 1 else jnp.concatenate(gparts, axis=0)

    for j in range(tile):
        a_copy(j).wait()
    for s in range(ROW_TILES):
        y = y0 if s < ROW_TILES // 2 else y1
        col = (s % (ROW_TILES // 2)) * LANES
        rows = pl.ds(s, tile, stride=MOE_PITCH)
        obuf[slot, rows, :] = abuf[rows, :] + y[:, col:col + LANES] * gcol

    @pl.when(step == last)
    def _():
        for_rows(lambda j: o_copy(idx_ref, j, slot).start())
        for_rows(lambda j: o_copy(idx_ref, j, slot).wait())
        for_rows(lambda j: x_copy(nidx_ref, j, other).wait())


def _moe(xp, hacc, idx, gate, wg, wu, wd, tile):
    cap = idx.shape[1]
    nt = cap // tile
    idx3 = idx.reshape(N_EXPERTS * nt, 1, tile)
    gate3 = gate.reshape(N_EXPERTS * nt, 1, tile)
    n_steps = N_EXPERTS * nt
    smem_idx = lambda im: pl.BlockSpec((1, 1, tile), im, memory_space=pltpu.SMEM)
    wspec = lambda shape: pl.BlockSpec((1,) + shape, lambda e, i: (e, 0, 0))
    any_spec = pl.BlockSpec(memory_space=pl.ANY)
    buf_bytes = tile * MOE_PITCH * LANES * 4
    est = 2 * 3 * D_MODEL * EXPERT_FF * 2 + 5 * buf_bytes + tile * D_MODEL * 2 + 8 * tile * D_MODEL * 4 \
        + 6 * tile * FF_CHUNK * 4 + tile * tile * 4 * 2 + (2 << 20)
    return pl.pallas_call(
        functools.partial(_moe_kernel, tile=tile),
        grid=(N_EXPERTS, nt),
        in_specs=[
            smem_idx(lambda e, i: (jnp.maximum(e * nt + i - 1, 0), 0, 0)),
            smem_idx(lambda e, i: (e * nt + i, 0, 0)),
            smem_idx(lambda e, i: (jnp.minimum(e * nt + i + 1, n_steps - 1), 0, 0)),
            pl.BlockSpec((1, 1, tile), lambda e, i: (e * nt + i, 0, 0)),
            wspec((D_MODEL, EXPERT_FF)),
            wspec((D_MODEL, EXPERT_FF)),
            wspec((EXPERT_FF, D_MODEL)),
            any_spec,
            any_spec,
        ],
        out_specs=any_spec,
        out_shape=jax.ShapeDtypeStruct(hacc.shape, F32),
        scratch_shapes=[
            pltpu.VMEM((2, tile * MOE_XPITCH, LANES), jnp.uint32),
            pltpu.VMEM((tile * MOE_PITCH, LANES), F32),
            pltpu.VMEM((2, tile * MOE_PITCH, LANES), F32),
            pltpu.VMEM((tile, D_MODEL), BF16),
            pltpu.SemaphoreType.DMA((2,)),
            pltpu.SemaphoreType.DMA(()),
            pltpu.SemaphoreType.DMA((2,)),
        ],
        input_output_aliases={8: 0},
        compiler_params=pltpu.CompilerParams(
            dimension_semantics=("arbitrary", "arbitrary"), vmem_limit_bytes=_vmem_limit(est)),
        name="moe",
    )(idx3, idx3, idx3, gate3, wg, wu, wd, xp, hacc)


def _ple_kernel(h_ref, p_ref, g_ref, wg_ref, wp_ref, gf_ref, o_ref, *, final):
    tm = o_ref.shape[0]
    h = jnp.concatenate([h_ref[pl.ds(s, tm, stride=ROW_TILES), :] for s in range(ROW_TILES)], axis=1)
    xn = _rmsnorm(h, g_ref[...]).astype(BF16)
    gate = _sigmoid(jnp.dot(xn, wg_ref[...], preferred_element_type=F32))
    proj = jnp.dot(p_ref[...].astype(BF16), wp_ref[...], preferred_element_type=F32)
    out = h + gate * proj
    if final:
        out = _rmsnorm(out, gf_ref[...])
    o_ref[...] = out


def _ple(h, p, layer, g, wg, wp, gf, seq_len, final):
    T = h.shape[0] // ROW_TILES
    tm = _token_tile(seq_len)
    row = lambda i: (i, 0)
    const = lambda i: (0, 0)
    est = 2 * (2 * tm * D_MODEL * 4 + tm * PLE_DIM * 4) + 2 * (D_MODEL + PLE_DIM) * D_MODEL * 2 + 6 * tm * D_MODEL * 4
    return pl.pallas_call(
        functools.partial(_ple_kernel, final=final),
        grid=(T // tm,),
        in_specs=[
            pl.BlockSpec((tm * ROW_TILES, LANES), row),
            pl.BlockSpec((None, tm, PLE_DIM), lambda i: (layer, i, 0)),
            pl.BlockSpec((1, D_MODEL), const),
            pl.BlockSpec((D_MODEL, D_MODEL), const),
            pl.BlockSpec((PLE_DIM, D_MODEL), const),
            pl.BlockSpec((1, D_MODEL), const),
        ],
        out_specs=pl.BlockSpec((tm, D_MODEL), row),
        out_shape=jax.ShapeDtypeStruct((T, D_MODEL), F32),
        compiler_params=pltpu.CompilerParams(
            dimension_semantics=("parallel",), vmem_limit_bytes=_vmem_limit(est)),
        name="ple",
    )(h, p, g, wg, wp, gf)


def _qk_column_order():
    half = HEAD_DIM // 2
    order = []
    for pair in range(RET_HEADS // 2):
        for part in range(2):
            for hh in range(2):
                base = (2 * pair + hh) * HEAD_DIM + part * half
                order.extend(range(base, base + half))
    return jnp.asarray(order, jnp.int32)


def _rope_tables(seq_len):
    half = HEAD_DIM // 2
    inv = 1.0 / (ROPE_BASE ** (jnp.arange(half, dtype=F32) / half))
    ang = jnp.arange(seq_len, dtype=F32)[:, None] * inv[None, :]
    cos, sin = jnp.cos(ang), jnp.sin(ang)
    return (jnp.concatenate([cos, cos, cos, cos], axis=1),
            jnp.concatenate([-sin, -sin, sin, sin], axis=1))


def _prep_layer(i, w_in, w_out, w_router, w_exp_gate, w_exp_up, w_exp_down, w_ple_gate, w_ple_proj):
    order = _qk_column_order()
    w = w_in[i]
    wq = w[:, SPLITS[0]:SPLITS[1]][:, order]
    wk = w[:, SPLITS[1]:SPLITS[2]][:, order]
    w_in_p = jnp.concatenate([w[:, :SPLITS[0]], wq, wk, w[:, SPLITS[2]:]], axis=1).astype(BF16)
    wr_t = w_router[i].T
    wr_hi = wr_t.astype(BF16)
    wr_lo = (wr_t - wr_hi.astype(F32)).astype(BF16)
    wo = w_out[i].astype(BF16)
    return dict(
        w_in=w_in_p, wo_a=wo[:CONV_CH], wo_b=wo[CONV_CH:], wr_hi=wr_hi, wr_lo=wr_lo,
        wg=w_exp_gate[i].astype(BF16), wu=w_exp_up[i].astype(BF16), wd=w_exp_down[i].astype(BF16),
        w_ple_gate=w_ple_gate[i].astype(BF16), w_ple_proj=w_ple_proj[i].astype(BF16))


def _moe_tile(cap, want=512):
    t = want
    while cap % t:
        t //= 2
    return t


def _trunk(x, p, layers, vecs, norm_final):
    batch, seq_len, _ = x.shape
    T = batch * seq_len
    cap = CAP_FACTOR * T // N_EXPERTS
    cos_t, sin_t = _rope_tables(seq_len)
    h = x.reshape(T, D_MODEL)
    depth = len(layers)
    p = p.reshape(depth, T, PLE_DIM)
    for i, (lw, lv) in enumerate(zip(layers, vecs)):
        u, q, k, v, og = _in_proj(h, lv["norm_mix"], lw["w_in"], cos_t, sin_t, seq_len)
        a_out = _conv(u, lv["conv_w"], lv["conv_b"], lv["conv_ln_g"], lv["conv_ln_b"], batch, seq_len)
        b_out = _retention(q, k, v, og, lv["lg_f"], lv["lg_b"], lv["gn_g"], batch, seq_len)
        xp, h_acc, aff_t = _out_proj(a_out, b_out, h, lw["wo_a"], lw["wo_b"], lv["norm_ffn"],
                                     lw["wr_hi"], lw["wr_lo"], seq_len)
        idx, gate = _select(aff_t, cap)
        h_acc = _moe(xp, h_acc, idx, gate, lw["wg"], lw["wu"], lw["wd"], _moe_tile(cap))
        h = _ple(h_acc, p, i, lv["norm_ple"], lw["w_ple_gate"], lw["w_ple_proj"],
                 norm_final, seq_len, final=(i == depth - 1))
    return h.reshape(batch, seq_len, D_MODEL)


def kernel(x_prompt, x_sample, p_prompt, p_sample, norm_mix, w_in, conv_w, conv_b, conv_ln_g, conv_ln_b,
           ret_log_gamma_fwd, ret_log_gamma_bwd, ret_gn_g, w_out, norm_ffn, w_router, w_exp_gate, w_exp_up,
           w_exp_down, norm_ple, w_ple_gate, w_ple_proj, norm_final):
    depth = w_in.shape[0]
    layers = [_prep_layer(i, w_in, w_out, w_router, w_exp_gate, w_exp_up, w_exp_down, w_ple_gate, w_ple_proj)
              for i in range(depth)]
    vecs = [dict(
        norm_mix=norm_mix[i].reshape(1, D_MODEL), conv_w=conv_w[i], conv_b=conv_b[i].reshape(1, CONV_CH),
        conv_ln_g=conv_ln_g[i].reshape(1, CONV_CH), conv_ln_b=conv_ln_b[i].reshape(1, CONV_CH),
        lg_f=ret_log_gamma_fwd[i], lg_b=ret_log_gamma_bwd[i], gn_g=ret_gn_g[i].reshape(1, RET_WIDTH),
        norm_ffn=norm_ffn[i].reshape(1, D_MODEL), norm_ple=norm_ple[i].reshape(1, D_MODEL))
        for i in range(depth)]
    gf = norm_final.reshape(1, D_MODEL)
    y_prompt = _trunk(x_prompt, p_prompt, layers, vecs, gf)
    y_sample = _trunk(x_sample, p_sample, layers, vecs, gf)
    return (y_prompt, y_sample)
```

```python
import functools
import math

import jax
import jax.numpy as jnp
from jax import lax
from jax.experimental import pallas as pl
from jax.experimental.pallas import tpu as pltpu

D_MODEL = 1024
CONV_CH = 512
RET_HEADS = 8
HEAD_DIM = 64
RET_WIDTH = RET_HEADS * HEAD_DIM
CONV_WIDTH = 31
CONV_PAD = CONV_WIDTH // 2
CHUNK = 128
ROPE_BASE = 10000.0
N_EXPERTS = 16
CAP_FACTOR = 2
EXPERT_FF = 2 * D_MODEL
PLE_DIM = 256
EPS = 1e-6
SPLITS = (2 * CONV_CH, 2 * CONV_CH + RET_WIDTH, 2 * CONV_CH + 2 * RET_WIDTH,
          2 * CONV_CH + 3 * RET_WIDTH, 2 * CONV_CH + 4 * RET_WIDTH)

LANES = 128
SUBLANES = 8
V7X_VMEM_BYTES = 64 * 1024 * 1024
V7X_VMEM_USABLE = 56 * 1024 * 1024
ROW_TILES = D_MODEL // LANES
PACK_TILES = ROW_TILES // 2

BF16 = jnp.bfloat16
F32 = jnp.float32
NT_DIMS = (((1,), (1,)), ((), ()))
TN_DIMS = (((0,), (0,)), ((), ()))


def _vmem_limit(estimate_bytes):
    return int(min(V7X_VMEM_USABLE, max(16 * 1024 * 1024, estimate_bytes)))


def _token_tile(seq_len, want=1024):
    tm = want
    while seq_len % tm:
        tm //= 2
    return tm


def _rmsnorm(x, g):
    y = x * lax.rsqrt(jnp.mean(x * x, axis=-1, keepdims=True) + EPS)
    return y * g


def _silu(x):
    return x * (1.0 / (1.0 + jnp.exp(-x)))


def _sigmoid(x):
    return 1.0 / (1.0 + jnp.exp(-x))


def _in_proj_kernel(h_ref, g_ref, w_ref, cq_ref, sq_ref, u_ref, q_ref, k_ref, v_ref, og_ref):
    xn = _rmsnorm(h_ref[...], g_ref[...]).astype(BF16)

    def seg(lo, hi):
        return jnp.dot(xn, w_ref[:, lo:hi], preferred_element_type=F32)

    u_ref[...] = seg(0, SPLITS[0]).astype(BF16)
    cos = cq_ref[...]
    sin = sq_ref[...]
    scale = HEAD_DIM ** -0.5
    for lo, ref, mul in ((SPLITS[0], q_ref, 1.0), (SPLITS[1], k_ref, scale)):
        full = seg(lo, lo + RET_WIDTH)
        for t in range(RET_WIDTH // LANES):
            x = full[:, t * LANES:(t + 1) * LANES]
            r = x * cos + pltpu.roll(x, LANES // 2, 1) * sin
            if mul != 1.0:
                r = r * mul
            ref[:, t * LANES:(t + 1) * LANES] = r.astype(BF16)
    v_ref[...] = seg(SPLITS[2], SPLITS[3]).astype(BF16)
    og_ref[...] = seg(SPLITS[3], SPLITS[4]).astype(BF16)


def _in_proj(h, g, w, cos_t, sin_t, seq_len):
    T = h.shape[0]
    tm = _token_tile(seq_len)
    nl = seq_len // tm
    row = lambda i: (i, 0)
    est = 2 * (tm * D_MODEL * 4 + tm * SPLITS[4] * 2 + 2 * tm * LANES * 4) + 2 * D_MODEL * SPLITS[4] * 2 \
        + 6 * tm * D_MODEL * 4
    return pl.pallas_call(
        _in_proj_kernel,
        grid=(T // tm,),
        in_specs=[
            pl.BlockSpec((tm, D_MODEL), row),
            pl.BlockSpec((1, D_MODEL), lambda i: (0, 0)),
            pl.BlockSpec((D_MODEL, SPLITS[4]), lambda i: (0, 0)),
            pl.BlockSpec((tm, LANES), lambda i: (i % nl, 0)),
            pl.BlockSpec((tm, LANES), lambda i: (i % nl, 0)),
        ],
        out_specs=[
            pl.BlockSpec((tm, 2 * CONV_CH), row),
            pl.BlockSpec((tm, RET_WIDTH), row),
            pl.BlockSpec((tm, RET_WIDTH), row),
            pl.BlockSpec((tm, RET_WIDTH), row),
            pl.BlockSpec((tm, RET_WIDTH), row),
        ],
        out_shape=[
            jax.ShapeDtypeStruct((T, 2 * CONV_CH), BF16),
            jax.ShapeDtypeStruct((T, RET_WIDTH), BF16),
            jax.ShapeDtypeStruct((T, RET_WIDTH), BF16),
            jax.ShapeDtypeStruct((T, RET_WIDTH), BF16),
            jax.ShapeDtypeStruct((T, RET_WIDTH), BF16),
        ],
        compiler_params=pltpu.CompilerParams(
            dimension_semantics=("parallel",), vmem_limit_bytes=_vmem_limit(est)),
        name="in_proj",
    )(h, g, w, cos_t, sin_t)


CONV_ROWS = 128
CONV_HALO = 16
CONV_WIN = CONV_ROWS + 2 * CONV_HALO


def _conv_kernel(u_ref, w_ref, b_ref, lg_ref, lb_ref, o_ref, hp_ref, cv_ref, *, seq_len):
    n_chunks = seq_len // CONV_ROWS
    zeros = jnp.zeros((CONV_HALO, CONV_CH), F32)
    hp_ref[0:CONV_HALO, :] = zeros
    hp_ref[CONV_HALO + seq_len:CONV_HALO + seq_len + CONV_HALO, :] = zeros

    def glu(ci, c):
        r0 = pl.multiple_of(ci * CONV_ROWS, CONV_ROWS)
        rows = u_ref[0, pl.ds(r0, CONV_ROWS), :].astype(F32)
        hp_ref[pl.ds(CONV_HALO + r0, CONV_ROWS), :] = rows[:, :CONV_CH] * _sigmoid(rows[:, CONV_CH:])
        return c

    lax.fori_loop(0, n_chunks, glu, 0)

    def conv(ci, c):
        r0 = pl.multiple_of(ci * CONV_ROWS, CONV_ROWS)
        for t in range(CONV_CH // LANES):
            cols = slice(t * LANES, (t + 1) * LANES)
            win = hp_ref[pl.ds(r0, CONV_WIN), cols]
            acc = jnp.zeros((CONV_ROWS, LANES), F32)
            for phase in range(SUBLANES):
                offs = [o for o in range(CONV_HALO - CONV_PAD, CONV_HALO - CONV_PAD + CONV_WIDTH)
                        if o % SUBLANES == phase]
                if not offs:
                    continue
                shifted = win if phase == 0 else pltpu.roll(win, CONV_WIN - phase, 0)
                for o in offs:
                    j = o - (CONV_HALO - CONV_PAD)
                    base = o - phase
                    acc = acc + shifted[base:base + CONV_ROWS] * w_ref[j:j + 1, cols]
            cv_ref[:, cols] = acc + b_ref[:, cols]
        y = cv_ref[...]
        mu = jnp.mean(y, axis=-1, keepdims=True)
        yc = y - mu
        var = jnp.mean(yc * yc, axis=-1, keepdims=True)
        z = yc * lax.rsqrt(var + EPS) * lg_ref[...] + lb_ref[...]
        o_ref[0, pl.ds(r0, CONV_ROWS), :] = _silu(z).astype(BF16)
        return c

    lax.fori_loop(0, n_chunks, conv, 0)


def _conv(u, w, b, lg, lb, batch, seq_len):
    u3 = u.reshape(batch, seq_len, 2 * CONV_CH)
    est = 2 * (seq_len * 2 * CONV_CH * 2 + seq_len * CONV_CH * 2) + (seq_len + 2 * CONV_HALO) * CONV_CH * 4 \
        + 8 * CONV_ROWS * CONV_CH * 4 + (1 << 20)
    const = lambda i: (0, 0)
    out = pl.pallas_call(
        functools.partial(_conv_kernel, seq_len=seq_len),
        grid=(batch,),
        in_specs=[
            pl.BlockSpec((1, seq_len, 2 * CONV_CH), lambda i: (i, 0, 0)),
            pl.BlockSpec((CONV_WIDTH, CONV_CH), const),
            pl.BlockSpec((1, CONV_CH), const),
            pl.BlockSpec((1, CONV_CH), const),
            pl.BlockSpec((1, CONV_CH), const),
        ],
        out_specs=pl.BlockSpec((1, seq_len, CONV_CH), lambda i: (i, 0, 0)),
        out_shape=jax.ShapeDtypeStruct((batch, seq_len, CONV_CH), BF16),
        scratch_shapes=[
            pltpu.VMEM((seq_len + 2 * CONV_HALO, CONV_CH), F32),
            pltpu.VMEM((CONV_ROWS, CONV_CH), F32),
        ],
        compiler_params=pltpu.CompilerParams(
            dimension_semantics=("parallel",), vmem_limit_bytes=_vmem_limit(est)),
        name="conv",
    )(u3, w, b, lg, lb)
    return out.reshape(batch * seq_len, CONV_CH)


def _dot2(x, m):
    hi = x.astype(BF16)
    lo = (x - hi.astype(F32)).astype(BF16)
    return jnp.dot(hi, m, preferred_element_type=F32) + jnp.dot(lo, m, preferred_element_type=F32)


RET_UNROLL = 16
RET_NORM_ROWS = 256
RET_NORM_UNROLL = 4


def _retention_kernel(lgf_ref, lgb_ref, q_ref, k_ref, v_ref, og_ref, gn_ref, o_ref, sf_ref, sb_ref, p_ref,
                      acc_ref, *, seq_len):
    nc = seq_len // CHUNK
    pair = pl.program_id(1)
    lgf = (lgf_ref[2 * pair], lgf_ref[2 * pair + 1])
    lgb = (lgb_ref[2 * pair], lgb_ref[2 * pair + 1])

    lane = lax.broadcasted_iota(jnp.int32, (CHUNK, LANES), 1)
    row = lax.broadcasted_iota(jnp.int32, (CHUNK, LANES), 0)
    rowf = row.astype(F32)
    qk_head1 = ((lane // (HEAD_DIM // 2)) % 2) == 1
    v_head1 = lane >= HEAD_DIM
    krow_head1 = ((row // (HEAD_DIM // 2)) % 2) == 1
    same_head = krow_head1 == v_head1

    def per_lane(pairvals, head1):
        return jnp.where(head1, pairvals[1], pairvals[0])

    lgf_qk = per_lane(lgf, qk_head1)
    lgb_qk = per_lane(lgb, qk_head1)
    lgf_v = per_lane(lgf, v_head1)
    lgb_v = per_lane(lgb, v_head1)
    zeta_f = jnp.exp(lgf_qk * (CHUNK - 1.0 - rowf))
    zeta_b = jnp.exp(lgb_qk * rowf)
    xi_f = jnp.exp(lgf_v * (rowf + 1.0))
    xi_b = jnp.exp(lgb_v * (CHUNK - rowf))
    dg_f = jnp.exp(per_lane(lgf, krow_head1) * float(CHUNK))
    dg_b = jnp.exp(per_lane(lgb, krow_head1) * float(CHUNK))
    diff = (row - lane).astype(F32)
    decay = []
    for hh in range(2):
        fwd = jnp.exp(lgf[hh] * jnp.maximum(diff, 0.0))
        bwd = jnp.exp(lgb[hh] * jnp.maximum(-diff, 0.0))
        decay.append(jnp.where(diff >= 0.0, fwd, bwd))
    decay2 = jnp.concatenate(decay, axis=0)
    group_mean = jnp.where((row >= HEAD_DIM) == v_head1, 1.0 / HEAD_DIM, 0.0).astype(BF16)

    def chunk_rows(c):
        return pl.ds(pl.multiple_of(c * CHUNK, CHUNK), CHUNK)

    def kv_update(kc, vc, zeta):
        kz = (kc.astype(F32) * zeta).astype(BF16)
        upd = lax.dot_general(kz, vc, TN_DIMS, preferred_element_type=F32)
        return jnp.where(same_head, upd, 0.0)

    def state_body(i, carry):
        sf, sb = carry
        cb = nc - 1 - i
        sf_ref[i] = sf.astype(BF16)
        sb_ref[cb] = sb.astype(BF16)
        rf = chunk_rows(i)
        rb = chunk_rows(cb)
        sf = sf * dg_f + kv_update(k_ref[rf, :], v_ref[rf, :], zeta_f)
        sb = sb * dg_b + kv_update(k_ref[rb, :], v_ref[rb, :], zeta_b)
        return sf, sb

    zero_state = jnp.zeros((LANES, LANES), F32)
    lax.fori_loop(0, nc, state_body, (zero_state, zero_state), unroll=RET_UNROLL)

    def score_body(c, carry):
        rows = chunk_rows(c)
        qc = q_ref[rows, :]
        zero = jnp.zeros_like(qc)
        q2 = jnp.concatenate([jnp.where(qk_head1, zero, qc), jnp.where(qk_head1, qc, zero)], axis=0)
        s = lax.dot_general(q2, k_ref[rows, :], NT_DIMS, preferred_element_type=F32)
        p = (s * decay2).astype(BF16)
        p_ref[c] = jnp.concatenate([p[:CHUNK], p[CHUNK:]], axis=1)
        return carry

    lax.fori_loop(0, nc, score_body, 0, unroll=RET_UNROLL)

    def value_body(c, carry):
        rows = chunk_rows(c)
        qc = q_ref[rows, :]
        vc = v_ref[rows, :]
        zero = jnp.zeros_like(vc)
        v2 = jnp.concatenate([jnp.where(v_head1, zero, vc), jnp.where(v_head1, vc, zero)], axis=0)
        o = jnp.dot(p_ref[c], v2, preferred_element_type=F32)
        o = o + xi_f * jnp.dot(qc, sf_ref[c], preferred_element_type=F32)
        o = o + xi_b * jnp.dot(qc, sb_ref[c], preferred_element_type=F32)
        acc_ref[rows, :] = o
        return carry

    lax.fori_loop(0, nc, value_body, 0, unroll=RET_UNROLL)

    def mean_body(b, carry):
        rows = pl.ds(pl.multiple_of(b * RET_NORM_ROWS, RET_NORM_ROWS), RET_NORM_ROWS)
        o = acc_ref[rows, :]
        acc_ref[rows, :] = o - _dot2(o, group_mean)
        return carry

    def norm_body(b, carry):
        rows = pl.ds(pl.multiple_of(b * RET_NORM_ROWS, RET_NORM_ROWS), RET_NORM_ROWS)
        oc = acc_ref[rows, :]
        var = _dot2(oc * oc, group_mean)
        on = oc * lax.rsqrt(var + EPS) * gn_ref[...]
        o_ref[rows, :] = (_silu(og_ref[rows, :].astype(F32)) * on).astype(BF16)
        return carry

    lax.fori_loop(0, seq_len // RET_NORM_ROWS, mean_body, 0, unroll=RET_NORM_UNROLL)
    lax.fori_loop(0, seq_len // RET_NORM_ROWS, norm_body, 0, unroll=RET_NORM_UNROLL)


def _retention(q, k, v, og, lgf, lgb, gn, batch, seq_len):
    T = batch * seq_len
    blk = pl.BlockSpec((seq_len, LANES), lambda b, p: (b, p))
    smem = pl.BlockSpec(memory_space=pltpu.SMEM)
    est = 2 * 5 * seq_len * LANES * 2 + 4 * (seq_len // CHUNK) * LANES * LANES * 2 + seq_len * LANES * 4 + (8 << 20)
    return pl.pallas_call(
        functools.partial(_retention_kernel, seq_len=seq_len),
        grid=(batch, RET_WIDTH // LANES),
        in_specs=[smem, smem, blk, blk, blk, blk, pl.BlockSpec((1, LANES), lambda b, p: (0, p))],
        out_specs=blk,
        out_shape=jax.ShapeDtypeStruct((T, RET_WIDTH), BF16),
        scratch_shapes=[pltpu.VMEM((seq_len // CHUNK, LANES, LANES), BF16),
                        pltpu.VMEM((seq_len // CHUNK, LANES, LANES), BF16),
                        pltpu.VMEM((seq_len // CHUNK, CHUNK, 2 * CHUNK), BF16),
                        pltpu.VMEM((seq_len, LANES), F32)],
        compiler_params=pltpu.CompilerParams(
            dimension_semantics=("parallel", "parallel"), vmem_limit_bytes=_vmem_limit(est)),
        name="retention",
    )(lgf, lgb, q, k, v, og, gn)


def _out_proj_kernel(a_ref, b_ref, h_ref, wa_ref, wb_ref, g_ref, wrh_ref, wrl_ref, xp_ref, hacc_ref, aff_ref):
    y = jnp.dot(a_ref[...], wa_ref[...], preferred_element_type=F32)
    y = y + jnp.dot(b_ref[...], wb_ref[...], preferred_element_type=F32)
    hn = h_ref[...] + y
    tm = hn.shape[0]
    for s in range(ROW_TILES):
        hacc_ref[pl.ds(s, tm, stride=ROW_TILES), :] = hn[:, s * LANES:(s + 1) * LANES]
    xn = _rmsnorm(hn, g_ref[...])
    x_hi = xn.astype(BF16)
    bits = pltpu.bitcast(x_hi.astype(F32), jnp.uint32)
    packed = (bits[:, D_MODEL // 2:] & jnp.uint32(0xFFFF0000)) | (bits[:, :D_MODEL // 2] >> 16)
    for s in range(PACK_TILES):
        xp_ref[pl.ds(s, tm, stride=PACK_TILES), :] = packed[:, s * LANES:(s + 1) * LANES]
    x_lo = (xn - x_hi.astype(F32)).astype(BF16)
    wr_hi = wrh_ref[...]
    logits = lax.dot_general(wr_hi, x_hi, NT_DIMS, preferred_element_type=F32)
    logits = logits + lax.dot_general(wr_hi, x_lo, NT_DIMS, preferred_element_type=F32)
    logits = logits + lax.dot_general(wrl_ref[...], x_hi, NT_DIMS, preferred_element_type=F32)
    e = jnp.exp(logits - jnp.max(logits, axis=0, keepdims=True))
    aff_ref[...] = e / jnp.sum(e, axis=0, keepdims=True)


def _out_proj(a, b, h, wa, wb, g, wr_hi, wr_lo, seq_len):
    T = h.shape[0]
    tm = _token_tile(seq_len)
    row = lambda i: (i, 0)
    const = lambda i: (0, 0)
    est = 2 * (2 * tm * CONV_CH * 2 + 3 * tm * D_MODEL * 4 + N_EXPERTS * tm * 4) + 2 * D_MODEL * D_MODEL * 2 \
        + 6 * tm * D_MODEL * 4
    return pl.pallas_call(
        _out_proj_kernel,
        grid=(T // tm,),
        in_specs=[
            pl.BlockSpec((tm, CONV_CH), row),
            pl.BlockSpec((tm, RET_WIDTH), row),
            pl.BlockSpec((tm, D_MODEL), row),
            pl.BlockSpec((CONV_CH, D_MODEL), const),
            pl.BlockSpec((RET_WIDTH, D_MODEL), const),
            pl.BlockSpec((1, D_MODEL), const),
            pl.BlockSpec((N_EXPERTS, D_MODEL), const),
            pl.BlockSpec((N_EXPERTS, D_MODEL), const),
        ],
        out_specs=[
            pl.BlockSpec((tm * PACK_TILES, LANES), row),
            pl.BlockSpec((tm * ROW_TILES, LANES), row),
            pl.BlockSpec((N_EXPERTS, tm), lambda i: (0, i)),
        ],
        out_shape=[
            jax.ShapeDtypeStruct((T * PACK_TILES, LANES), jnp.uint32),
            jax.ShapeDtypeStruct((T * ROW_TILES, LANES), F32),
            jax.ShapeDtypeStruct((N_EXPERTS, T), F32),
        ],
        compiler_params=pltpu.CompilerParams(
            dimension_semantics=("parallel",), vmem_limit_bytes=_vmem_limit(est)),
        name="out_proj",
    )(a, b, h, wa, wb, g, wr_hi, wr_lo)


def _select_kernel(aff_ref, idx_ref, gate_ref, *, cap):
    bits = pltpu.bitcast(aff_ref[0], jnp.int32)
    R = bits.shape[0]
    n_tok = R * LANES
    lane = lax.broadcasted_iota(jnp.int32, (R, LANES), 1)
    row = lax.broadcasted_iota(jnp.int32, (R, LANES), 0)

    def count(mask):
        ones_f = jnp.where(mask, 1.0, 0.0)
        return jnp.sum(jnp.sum(ones_f, axis=0, keepdims=True), axis=1, keepdims=True)

    thr = jnp.zeros((1, 1), jnp.int32)
    for bit in range(30, -1, -1):
        cand = thr | (1 << bit)
        thr = jnp.where(count(bits >= cand) >= cap, cand, thr)

    tri = (lax.broadcasted_iota(jnp.int32, (LANES, LANES), 0)
           <= lax.broadcasted_iota(jnp.int32, (LANES, LANES), 1)).astype(BF16)
    ones = jnp.ones((LANES, LANES), BF16)

    def shift_rows(x, s):
        return jnp.where(row >= s, pltpu.roll(x, s, 0), 0)

    def exclusive_rank(mask):
        m = mask.astype(BF16)
        incl = jnp.dot(m, tri, preferred_element_type=F32).astype(jnp.int32)
        tot = jnp.dot(m, ones, preferred_element_type=F32).astype(jnp.int32)
        acc = tot
        s = 1
        while s < R:
            acc = acc + shift_rows(acc, s)
            s *= 2
        return acc - tot + incl - mask.astype(jnp.int32)

    gt = bits > thr
    eq = bits == thr
    need = cap - count(gt)
    sel = gt | (eq & (exclusive_rank(eq) < need))
    tok = row * LANES + lane
    dist = jnp.where(sel, tok - exclusive_rank(sel), -1)

    def pull(x, a, fill):
        if a < LANES:
            near = pltpu.roll(x, LANES - a, 1)
            far = pltpu.roll(near, R - 1, 0) if R > 1 else near
            out = jnp.where(lane < LANES - a, near, far)
            valid = (row < R - 1) | (lane < LANES - a)
        else:
            s = a // LANES
            out = pltpu.roll(x, R - s, 0)
            valid = row < R - s
        return jnp.where(valid, out, fill)

    gbits = bits
    k = 0
    while (1 << k) < n_tok:
        a = 1 << k
        d_in = pull(dist, a, -1)
        t_in = pull(tok, a, 0)
        g_in = pull(gbits, a, 0)
        moves_in = (d_in >= 0) & (((d_in >> k) & 1) == 1)
        stays = (dist >= 0) & (((dist >> k) & 1) == 0)
        tok = jnp.where(moves_in, t_in, tok)
        gbits = jnp.where(moves_in, g_in, gbits)
        dist = jnp.where(moves_in, d_in, jnp.where(stays, dist, -1))
        k += 1

    rows_out = cap // LANES
    idx_ref[0] = tok[:rows_out]
    gate_ref[0] = pltpu.bitcast(gbits[:rows_out], F32)


def _select(aff_t, cap):
    T = aff_t.shape[1]
    R = T // LANES
    rows_out = cap // LANES
    aff3 = aff_t.reshape(N_EXPERTS, R, LANES)
    est = 40 * R * LANES * 4 + (2 << 20)
    idx, gate = pl.pallas_call(
        functools.partial(_select_kernel, cap=cap),
        grid=(N_EXPERTS,),
        in_specs=[pl.BlockSpec((1, R, LANES), lambda e: (e, 0, 0))],
        out_specs=[pl.BlockSpec((1, rows_out, LANES), lambda e: (e, 0, 0)),
                   pl.BlockSpec((1, rows_out, LANES), lambda e: (e, 0, 0))],
        out_shape=[jax.ShapeDtypeStruct((N_EXPERTS, rows_out, LANES), jnp.int32),
                   jax.ShapeDtypeStruct((N_EXPERTS, rows_out, LANES), F32)],
        compiler_params=pltpu.CompilerParams(
            dimension_semantics=("parallel",), vmem_limit_bytes=_vmem_limit(est)),
        name="select",
    )(aff3)
    return idx.reshape(N_EXPERTS, cap), gate.reshape(N_EXPERTS, cap)


FF_CHUNK = 512


MOE_PITCH = ROW_TILES + 1
MOE_XPITCH = PACK_TILES + 1
MOE_PIECES = 6
GATE_ROWS = 512


def _moe_kernel(pidx_ref, idx_ref, nidx_ref, gate_ref, wg_ref, wu_ref, wd_ref, xp_hbm, acc_in_hbm,
                acc_hbm, xbuf, abuf, obuf, xn_ref, xsem, asem, osem, *, tile):
    del acc_in_hbm
    nt = pl.num_programs(1)
    step = pl.program_id(0) * nt + pl.program_id(1)
    last = pl.num_programs(0) * nt - 1
    slot = step % 2
    other = 1 - slot

    def token_rows(ref, t):
        return ref.at[pl.ds(pl.multiple_of(t * ROW_TILES, ROW_TILES), ROW_TILES), :]

    def buf_rows(j):
        return pl.ds(j * MOE_PITCH, ROW_TILES)

    def x_copy(ids, j, sl):
        src = xp_hbm.at[pl.ds(pl.multiple_of(ids[0, 0, j] * PACK_TILES, PACK_TILES), PACK_TILES), :]
        return pltpu.make_async_copy(src, xbuf.at[sl, pl.ds(j * MOE_XPITCH, PACK_TILES), :], xsem.at[sl])

    def a_copy(j):
        return pltpu.make_async_copy(token_rows(acc_hbm, idx_ref[0, 0, j]), abuf.at[buf_rows(j), :], asem)

    def o_copy(ids, j, sl):
        return pltpu.make_async_copy(obuf.at[sl, buf_rows(j), :], token_rows(acc_hbm, ids[0, 0, j]), osem.at[sl])

    def seed_copy(j):
        return pltpu.make_async_copy(token_rows(acc_hbm, idx_ref[0, 0, j]), obuf.at[other, buf_rows(j), :],
                                     osem.at[other])

    def for_rows(fn):
        def body(j, c):
            fn(j)
            return c
        lax.fori_loop(0, tile, body, 0)

    @pl.when(step == 0)
    def _():
        for_rows(lambda j: x_copy(idx_ref, j, slot).start())
        for_rows(lambda j: seed_copy(j).start())
        for_rows(lambda j: seed_copy(j).wait())

    for j in range(tile):
        x_copy(idx_ref, j, slot).wait()
    for s in range(PACK_TILES):
        w = xbuf[slot, pl.ds(s, tile, stride=MOE_XPITCH), :]
        lo = pltpu.bitcast(w << 16, F32)
        hi = pltpu.bitcast(w & jnp.uint32(0xFFFF0000), F32)
        xn_ref[:, s * LANES:(s + 1) * LANES] = lo.astype(BF16)
        xn_ref[:, D_MODEL // 2 + s * LANES:D_MODEL // 2 + (s + 1) * LANES] = hi.astype(BF16)

    issue_plan = {
        0: [functools.partial(lambda j: o_copy(pidx_ref, j, other).start(priority=j % 2), j) for j in range(tile)],
        1: [functools.partial(lambda j: x_copy(nidx_ref, j, other).start(priority=j % 2), j) for j in range(tile)],
        2: [functools.partial(lambda j: a_copy(j).start(priority=j % 2), j) for j in range(tile)],
    }
    half_ff = FF_CHUNK // 2
    half_d = D_MODEL // 2
    n_ff = EXPERT_FF // FF_CHUNK
    y0 = jnp.zeros((tile, half_d), F32)
    y1 = jnp.zeros((tile, half_d), F32)
    for c in range(n_ff):
        batch = issue_plan.get(c, [])
        per = -(-len(batch) // MOE_PIECES)

        def issue(k):
            for fn in batch[k * per:(k + 1) * per]:
                fn()

        lo = c * FF_CHUNK
        mid = lo + half_ff
        hi = lo + FF_CHUNK
        issue(0)
        hg0 = jnp.dot(xn_ref[...], wg_ref[0, :, lo:mid], preferred_element_type=F32)
        issue(1)
        hg1 = jnp.dot(xn_ref[...], wg_ref[0, :, mid:hi], preferred_element_type=F32)
        issue(2)
        hu0 = jnp.dot(xn_ref[...], wu_ref[0, :, lo:mid], preferred_element_type=F32)
        issue(3)
        hu1 = jnp.dot(xn_ref[...], wu_ref[0, :, mid:hi], preferred_element_type=F32)
        hid = jnp.concatenate([_silu(hg0) * hu0, _silu(hg1) * hu1], axis=1).astype(BF16)
        issue(4)
        y0 = y0 + jnp.dot(hid, wd_ref[0, lo:hi, :half_d], preferred_element_type=F32)
        issue(5)
        y1 = y1 + jnp.dot(hid, wd_ref[0, lo:hi, half_d:], preferred_element_type=F32)
        if c == 1:
            for j in range(tile):
                o_copy(pidx_ref, j, other).wait()
    gparts = []
    for r0 in range(0, tile, GATE_ROWS):
        rr = min(GATE_ROWS, tile - r0)
        eye = (lax.broadcasted_iota(jnp.int32, (rr, tile), 0) + r0) == lax.broadcasted_iota(jnp.int32, (rr, tile), 1)
        gparts.append(jnp.sum(jnp.where(eye, gate_ref[0], 0.0), axis=1, keepdims=True))
    gcol = gparts[0] if len(gparts) == 1 else jnp.concatenate(gparts, axis=0)

    for j in range(tile):
        a_copy(j).wait()
    for s in range(ROW_TILES):
        y = y0 if s < ROW_TILES // 2 else y1
        col = (s % (ROW_TILES // 2)) * LANES
        rows = pl.ds(s, tile, stride=MOE_PITCH)
        obuf[slot, rows, :] = abuf[rows, :] + y[:, col:col + LANES] * gcol

    @pl.when(step == last)
    def _():
        for_rows(lambda j: o_copy(idx_ref, j, slot).start())
        for_rows(lambda j: o_copy(idx_ref, j, slot).wait())
        for_rows(lambda j: x_copy(nidx_ref, j, other).wait())


def _moe(xp, hacc, idx, gate, wg, wu, wd, tile):
    cap = idx.shape[1]
    nt = cap // tile
    idx3 = idx.reshape(N_EXPERTS * nt, 1, tile)
    gate3 = gate.reshape(N_EXPERTS * nt, 1, tile)
    n_steps = N_EXPERTS * nt
    smem_idx = lambda im: pl.BlockSpec((1, 1, tile), im, memory_space=pltpu.SMEM)
    wspec = lambda shape: pl.BlockSpec((1,) + shape, lambda e, i: (e, 0, 0), pipeline_mode=pl.Buffered(1))
    any_spec = pl.BlockSpec(memory_space=pl.ANY)
    buf_bytes = tile * MOE_PITCH * LANES * 4
    est = 2 * 3 * D_MODEL * EXPERT_FF * 2 + 5 * buf_bytes + tile * D_MODEL * 2 + 8 * tile * D_MODEL * 4 \
        + 6 * tile * FF_CHUNK * 4 + tile * tile * 4 * 2 + (2 << 20)
    return pl.pallas_call(
        functools.partial(_moe_kernel, tile=tile),
        grid=(N_EXPERTS, nt),
        in_specs=[
            smem_idx(lambda e, i: (jnp.maximum(e * nt + i - 1, 0), 0, 0)),
            smem_idx(lambda e, i: (e * nt + i, 0, 0)),
            smem_idx(lambda e, i: (jnp.minimum(e * nt + i + 1, n_steps - 1), 0, 0)),
            pl.BlockSpec((1, 1, tile), lambda e, i: (e * nt + i, 0, 0)),
            wspec((D_MODEL, EXPERT_FF)),
            wspec((D_MODEL, EXPERT_FF)),
            wspec((EXPERT_FF, D_MODEL)),
            any_spec,
            any_spec,
        ],
        out_specs=any_spec,
        out_shape=jax.ShapeDtypeStruct(hacc.shape, F32),
        scratch_shapes=[
            pltpu.VMEM((2, tile * MOE_XPITCH, LANES), jnp.uint32),
            pltpu.VMEM((tile * MOE_PITCH, LANES), F32),
            pltpu.VMEM((2, tile * MOE_PITCH, LANES), F32),
            pltpu.VMEM((tile, D_MODEL), BF16),
            pltpu.SemaphoreType.DMA((2,)),
            pltpu.SemaphoreType.DMA(()),
            pltpu.SemaphoreType.DMA((2,)),
        ],
        input_output_aliases={8: 0},
        compiler_params=pltpu.CompilerParams(
            dimension_semantics=("arbitrary", "arbitrary"), vmem_limit_bytes=_vmem_limit(est)),
        name="moe",
    )(idx3, idx3, idx3, gate3, wg, wu, wd, xp, hacc)


def _ple_kernel(h_ref, p_ref, g_ref, wg_ref, wp_ref, gf_ref, o_ref, *, final):
    tm = o_ref.shape[0]
    h = jnp.concatenate([h_ref[pl.ds(s, tm, stride=ROW_TILES), :] for s in range(ROW_TILES)], axis=1)
    xn = _rmsnorm(h, g_ref[...]).astype(BF16)
    gate = _sigmoid(jnp.dot(xn, wg_ref[...], preferred_element_type=F32))
    proj = jnp.dot(p_ref[...].astype(BF16), wp_ref[...], preferred_element_type=F32)
    out = h + gate * proj
    if final:
        out = _rmsnorm(out, gf_ref[...])
    o_ref[...] = out


def _ple(h, p, layer, g, wg, wp, gf, seq_len, final):
    T = h.shape[0] // ROW_TILES
    tm = _token_tile(seq_len)
    row = lambda i: (i, 0)
    const = lambda i: (0, 0)
    est = 2 * (2 * tm * D_MODEL * 4 + tm * PLE_DIM * 4) + 2 * (D_MODEL + PLE_DIM) * D_MODEL * 2 + 6 * tm * D_MODEL * 4
    return pl.pallas_call(
        functools.partial(_ple_kernel, final=final),
        grid=(T // tm,),
        in_specs=[
            pl.BlockSpec((tm * ROW_TILES, LANES), row),
            pl.BlockSpec((None, tm, PLE_DIM), lambda i: (layer, i, 0)),
            pl.BlockSpec((1, D_MODEL), const),
            pl.BlockSpec((D_MODEL, D_MODEL), const),
            pl.BlockSpec((PLE_DIM, D_MODEL), const),
            pl.BlockSpec((1, D_MODEL), const),
        ],
        out_specs=pl.BlockSpec((tm, D_MODEL), row),
        out_shape=jax.ShapeDtypeStruct((T, D_MODEL), F32),
        compiler_params=pltpu.CompilerParams(
            dimension_semantics=("parallel",), vmem_limit_bytes=_vmem_limit(est)),
        name="ple",
    )(h, p, g, wg, wp, gf)


def _qk_column_order():
    half = HEAD_DIM // 2
    order = []
    for pair in range(RET_HEADS // 2):
        for part in range(2):
            for hh in range(2):
                base = (2 * pair + hh) * HEAD_DIM + part * half
                order.extend(range(base, base + half))
    return jnp.asarray(order, jnp.int32)


def _rope_tables(seq_len):
    half = HEAD_DIM // 2
    inv = 1.0 / (ROPE_BASE ** (jnp.arange(half, dtype=F32) / half))
    ang = jnp.arange(seq_len, dtype=F32)[:, None] * inv[None, :]
    cos, sin = jnp.cos(ang), jnp.sin(ang)
    return (jnp.concatenate([cos, cos, cos, cos], axis=1),
            jnp.concatenate([-sin, -sin, sin, sin], axis=1))


def _prep_layer(i, w_in, w_out, w_router, w_exp_gate, w_exp_up, w_exp_down, w_ple_gate, w_ple_proj):
    order = _qk_column_order()
    w = w_in[i]
    wq = w[:, SPLITS[0]:SPLITS[1]][:, order]
    wk = w[:, SPLITS[1]:SPLITS[2]][:, order]
    w_in_p = jnp.concatenate([w[:, :SPLITS[0]], wq, wk, w[:, SPLITS[2]:]], axis=1).astype(BF16)
    wr_t = w_router[i].T
    wr_hi = wr_t.astype(BF16)
    wr_lo = (wr_t - wr_hi.astype(F32)).astype(BF16)
    wo = w_out[i].astype(BF16)
    return dict(
        w_in=w_in_p, wo_a=wo[:CONV_CH], wo_b=wo[CONV_CH:], wr_hi=wr_hi, wr_lo=wr_lo,
        wg=w_exp_gate[i].astype(BF16), wu=w_exp_up[i].astype(BF16), wd=w_exp_down[i].astype(BF16),
        w_ple_gate=w_ple_gate[i].astype(BF16), w_ple_proj=w_ple_proj[i].astype(BF16))


def _moe_tile(cap, want=1024):
    t = want
    while cap % t:
        t //= 2
    return t


def _trunk(x, p, layers, vecs, norm_final):
    batch, seq_len, _ = x.shape
    T = batch * seq_len
    cap = CAP_FACTOR * T // N_EXPERTS
    cos_t, sin_t = _rope_tables(seq_len)
    h = x.reshape(T, D_MODEL)
    depth = len(layers)
    p = p.reshape(depth, T, PLE_DIM)
    for i, (lw, lv) in enumerate(zip(layers, vecs)):
        u, q, k, v, og = _in_proj(h, lv["norm_mix"], lw["w_in"], cos_t, sin_t, seq_len)
        a_out = _conv(u, lv["conv_w"], lv["conv_b"], lv["conv_ln_g"], lv["conv_ln_b"], batch, seq_len)
        b_out = _retention(q, k, v, og, lv["lg_f"], lv["lg_b"], lv["gn_g"], batch, seq_len)
        xp, h_acc, aff_t = _out_proj(a_out, b_out, h, lw["wo_a"], lw["wo_b"], lv["norm_ffn"],
                                     lw["wr_hi"], lw["wr_lo"], seq_len)
        idx, gate = _select(aff_t, cap)
        h_acc = _moe(xp, h_acc, idx, gate, lw["wg"], lw["wu"], lw["wd"], _moe_tile(cap))
        h = _ple(h_acc, p, i, lv["norm_ple"], lw["w_ple_gate"], lw["w_ple_proj"],
                 norm_final, seq_len, final=(i == depth - 1))
    return h.reshape(batch, seq_len, D_MODEL)


def kernel(x_prompt, x_sample, p_prompt, p_sample, norm_mix, w_in, conv_w, conv_b, conv_ln_g, conv_ln_b,
           ret_log_gamma_fwd, ret_log_gamma_bwd, ret_gn_g, w_out, norm_ffn, w_router, w_exp_gate, w_exp_up,
           w_exp_down, norm_ple, w_ple_gate, w_ple_proj, norm_final):
    depth = w_in.shape[0]
    layers = [_prep_layer(i, w_in, w_out, w_router, w_exp_gate, w_exp_up, w_exp_down, w_ple_gate, w_ple_proj)
              for i in range(depth)]
    vecs = [dict(
        norm_mix=norm_mix[i].reshape(1, D_MODEL), conv_w=conv_w[i], conv_b=conv_b[i].reshape(1, CONV_CH),
        conv_ln_g=conv_ln_g[i].reshape(1, CONV_CH), conv_ln_b=conv_ln_b[i].reshape(1, CONV_CH),
        lg_f=ret_log_gamma_fwd[i], lg_b=ret_log_gamma_bwd[i], gn_g=ret_gn_g[i].reshape(1, RET_WIDTH),
        norm_ffn=norm_ffn[i].reshape(1, D_MODEL), norm_ple=norm_ple[i].reshape(1, D_MODEL))
        for i in range(depth)]
    gf = norm_final.reshape(1, D_MODEL)
    y_prompt = _trunk(x_prompt, p_prompt, layers, vecs, gf)
    y_sample = _trunk(x_sample, p_sample, layers, vecs, gf)
    return (y_prompt, y_sample)
```

```python
import functools
import math

import jax
import jax.numpy as jnp
from jax import lax
from jax.experimental import pallas as pl
from jax.experimental.pallas import tpu as pltpu

D_MODEL = 1024
CONV_CH = 512
RET_HEADS = 8
HEAD_DIM = 64
RET_WIDTH = RET_HEADS * HEAD_DIM
CONV_WIDTH = 31
CONV_PAD = CONV_WIDTH // 2
CHUNK = 128
ROPE_BASE = 10000.0
N_EXPERTS = 16
CAP_FACTOR = 2
EXPERT_FF = 2 * D_MODEL
PLE_DIM = 256
EPS = 1e-6
SPLITS = (2 * CONV_CH, 2 * CONV_CH + RET_WIDTH, 2 * CONV_CH + 2 * RET_WIDTH,
          2 * CONV_CH + 3 * RET_WIDTH, 2 * CONV_CH + 4 * RET_WIDTH)

LANES = 128
SUBLANES = 8
V7X_VMEM_BYTES = 64 * 1024 * 1024
V7X_VMEM_USABLE = 56 * 1024 * 1024
ROW_TILES = D_MODEL // LANES
PACK_TILES = ROW_TILES // 2

BF16 = jnp.bfloat16
F32 = jnp.float32
NT_DIMS = (((1,), (1,)), ((), ()))
TN_DIMS = (((0,), (0,)), ((), ()))


def _vmem_limit(estimate_bytes):
    return int(min(V7X_VMEM_USABLE, max(16 * 1024 * 1024, estimate_bytes)))


def _token_tile(seq_len, want=1024):
    tm = want
    while seq_len % tm:
        tm //= 2
    return tm


def _rmsnorm(x, g):
    y = x * lax.rsqrt(jnp.mean(x * x, axis=-1, keepdims=True) + EPS)
    return y * g


def _silu(x):
    return x * (1.0 / (1.0 + jnp.exp(-x)))


def _sigmoid(x):
    return 1.0 / (1.0 + jnp.exp(-x))


def _in_proj_kernel(h_ref, g_ref, w_ref, cq_ref, sq_ref, u_ref, q_ref, k_ref, v_ref, og_ref):
    xn = _rmsnorm(h_ref[...], g_ref[...]).astype(BF16)

    def seg(lo, hi):
        return jnp.dot(xn, w_ref[:, lo:hi], preferred_element_type=F32)

    u_ref[...] = seg(0, SPLITS[0]).astype(BF16)
    cos = cq_ref[...]
    sin = sq_ref[...]
    scale = HEAD_DIM ** -0.5
    for lo, ref, mul in ((SPLITS[0], q_ref, 1.0), (SPLITS[1], k_ref, scale)):
        full = seg(lo, lo + RET_WIDTH)
        for t in range(RET_WIDTH // LANES):
            x = full[:, t * LANES:(t + 1) * LANES]
            r = x * cos + pltpu.roll(x, LANES // 2, 1) * sin
            if mul != 1.0:
                r = r * mul
            ref[:, t * LANES:(t + 1) * LANES] = r.astype(BF16)
    v_ref[...] = seg(SPLITS[2], SPLITS[3]).astype(BF16)
    og_ref[...] = seg(SPLITS[3], SPLITS[4]).astype(BF16)


def _in_proj(h, g, w, cos_t, sin_t, seq_len):
    T = h.shape[0]
    tm = _token_tile(seq_len)
    nl = seq_len // tm
    row = lambda i: (i, 0)
    est = 2 * (tm * D_MODEL * 4 + tm * SPLITS[4] * 2 + 2 * tm * LANES * 4) + 2 * D_MODEL * SPLITS[4] * 2 \
        + 6 * tm * D_MODEL * 4
    return pl.pallas_call(
        _in_proj_kernel,
        grid=(T // tm,),
        in_specs=[
            pl.BlockSpec((tm, D_MODEL), row),
            pl.BlockSpec((1, D_MODEL), lambda i: (0, 0)),
            pl.BlockSpec((D_MODEL, SPLITS[4]), lambda i: (0, 0)),
            pl.BlockSpec((tm, LANES), lambda i: (i % nl, 0)),
            pl.BlockSpec((tm, LANES), lambda i: (i % nl, 0)),
        ],
        out_specs=[
            pl.BlockSpec((tm, 2 * CONV_CH), row),
            pl.BlockSpec((tm, RET_WIDTH), row),
            pl.BlockSpec((tm, RET_WIDTH), row),
            pl.BlockSpec((tm, RET_WIDTH), row),
            pl.BlockSpec((tm, RET_WIDTH), row),
        ],
        out_shape=[
            jax.ShapeDtypeStruct((T, 2 * CONV_CH), BF16),
            jax.ShapeDtypeStruct((T, RET_WIDTH), BF16),
            jax.ShapeDtypeStruct((T, RET_WIDTH), BF16),
            jax.ShapeDtypeStruct((T, RET_WIDTH), BF16),
            jax.ShapeDtypeStruct((T, RET_WIDTH), BF16),
        ],
        compiler_params=pltpu.CompilerParams(
            dimension_semantics=("parallel",), vmem_limit_bytes=_vmem_limit(est)),
        name="in_proj",
    )(h, g, w, cos_t, sin_t)


CONV_ROWS = 128
CONV_HALO = 16
CONV_WIN = CONV_ROWS + 2 * CONV_HALO


def _conv_kernel(u_ref, w_ref, b_ref, lg_ref, lb_ref, o_ref, hp_ref, cv_ref, *, seq_len):
    n_chunks = seq_len // CONV_ROWS
    zeros = jnp.zeros((CONV_HALO, CONV_CH), F32)
    hp_ref[0:CONV_HALO, :] = zeros
    hp_ref[CONV_HALO + seq_len:CONV_HALO + seq_len + CONV_HALO, :] = zeros

    def glu(ci, c):
        r0 = pl.multiple_of(ci * CONV_ROWS, CONV_ROWS)
        rows = u_ref[0, pl.ds(r0, CONV_ROWS), :].astype(F32)
        hp_ref[pl.ds(CONV_HALO + r0, CONV_ROWS), :] = rows[:, :CONV_CH] * _sigmoid(rows[:, CONV_CH:])
        return c

    lax.fori_loop(0, n_chunks, glu, 0)

    def conv(ci, c):
        r0 = pl.multiple_of(ci * CONV_ROWS, CONV_ROWS)
        for t in range(CONV_CH // LANES):
            cols = slice(t * LANES, (t + 1) * LANES)
            win = hp_ref[pl.ds(r0, CONV_WIN), cols]
            acc = jnp.zeros((CONV_ROWS, LANES), F32)
            for phase in range(SUBLANES):
                offs = [o for o in range(CONV_HALO - CONV_PAD, CONV_HALO - CONV_PAD + CONV_WIDTH)
                        if o % SUBLANES == phase]
                if not offs:
                    continue
                shifted = win if phase == 0 else pltpu.roll(win, CONV_WIN - phase, 0)
                for o in offs:
                    j = o - (CONV_HALO - CONV_PAD)
                    base = o - phase
                    acc = acc + shifted[base:base + CONV_ROWS] * w_ref[j:j + 1, cols]
            cv_ref[:, cols] = acc + b_ref[:, cols]
        y = cv_ref[...]
        mu = jnp.mean(y, axis=-1, keepdims=True)
        yc = y - mu
        var = jnp.mean(yc * yc, axis=-1, keepdims=True)
        z = yc * lax.rsqrt(var + EPS) * lg_ref[...] + lb_ref[...]
        o_ref[0, pl.ds(r0, CONV_ROWS), :] = _silu(z).astype(BF16)
        return c

    lax.fori_loop(0, n_chunks, conv, 0)


def _conv(u, w, b, lg, lb, batch, seq_len):
    u3 = u.reshape(batch, seq_len, 2 * CONV_CH)
    est = 2 * (seq_len * 2 * CONV_CH * 2 + seq_len * CONV_CH * 2) + (seq_len + 2 * CONV_HALO) * CONV_CH * 4 \
        + 8 * CONV_ROWS * CONV_CH * 4 + (1 << 20)
    const = lambda i: (0, 0)
    out = pl.pallas_call(
        functools.partial(_conv_kernel, seq_len=seq_len),
        grid=(batch,),
        in_specs=[
            pl.BlockSpec((1, seq_len, 2 * CONV_CH), lambda i: (i, 0, 0)),
            pl.BlockSpec((CONV_WIDTH, CONV_CH), const),
            pl.BlockSpec((1, CONV_CH), const),
            pl.BlockSpec((1, CONV_CH), const),
            pl.BlockSpec((1, CONV_CH), const),
        ],
        out_specs=pl.BlockSpec((1, seq_len, CONV_CH), lambda i: (i, 0, 0)),
        out_shape=jax.ShapeDtypeStruct((batch, seq_len, CONV_CH), BF16),
        scratch_shapes=[
            pltpu.VMEM((seq_len + 2 * CONV_HALO, CONV_CH), F32),
            pltpu.VMEM((CONV_ROWS, CONV_CH), F32),
        ],
        compiler_params=pltpu.CompilerParams(
            dimension_semantics=("parallel",), vmem_limit_bytes=_vmem_limit(est)),
        name="conv",
    )(u3, w, b, lg, lb)
    return out.reshape(batch * seq_len, CONV_CH)


def _dot2(x, m):
    hi = x.astype(BF16)
    lo = (x - hi.astype(F32)).astype(BF16)
    return jnp.dot(hi, m, preferred_element_type=F32) + jnp.dot(lo, m, preferred_element_type=F32)


RET_UNROLL = 16
RET_NORM_ROWS = 256
RET_NORM_UNROLL = 4


def _retention_kernel(lgf_ref, lgb_ref, q_ref, k_ref, v_ref, og_ref, gn_ref, o_ref, sf_ref, sb_ref, p_ref,
                      acc_ref, *, seq_len):
    nc = seq_len // CHUNK
    pair = pl.program_id(1)
    lgf = (lgf_ref[2 * pair], lgf_ref[2 * pair + 1])
    lgb = (lgb_ref[2 * pair], lgb_ref[2 * pair + 1])

    lane = lax.broadcasted_iota(jnp.int32, (CHUNK, LANES), 1)
    row = lax.broadcasted_iota(jnp.int32, (CHUNK, LANES), 0)
    rowf = row.astype(F32)
    qk_head1 = ((lane // (HEAD_DIM // 2)) % 2) == 1
    v_head1 = lane >= HEAD_DIM
    krow_head1 = ((row // (HEAD_DIM // 2)) % 2) == 1
    same_head = krow_head1 == v_head1

    def per_lane(pairvals, head1):
        return jnp.where(head1, pairvals[1], pairvals[0])

    lgf_qk = per_lane(lgf, qk_head1)
    lgb_qk = per_lane(lgb, qk_head1)
    lgf_v = per_lane(lgf, v_head1)
    lgb_v = per_lane(lgb, v_head1)
    zeta_f = jnp.exp(lgf_qk * (CHUNK - 1.0 - rowf))
    zeta_b = jnp.exp(lgb_qk * rowf)
    xi_f = jnp.exp(lgf_v * (rowf + 1.0))
    xi_b = jnp.exp(lgb_v * (CHUNK - rowf))
    dg_f = jnp.exp(per_lane(lgf, krow_head1) * float(CHUNK))
    dg_b = jnp.exp(per_lane(lgb, krow_head1) * float(CHUNK))
    diff = (row - lane).astype(F32)
    decay = []
    for hh in range(2):
        fwd = jnp.exp(lgf[hh] * jnp.maximum(diff, 0.0))
        bwd = jnp.exp(lgb[hh] * jnp.maximum(-diff, 0.0))
        decay.append(jnp.where(diff >= 0.0, fwd, bwd))
    decay2 = jnp.concatenate(decay, axis=0)
    group_mean = jnp.where((row >= HEAD_DIM) == v_head1, 1.0 / HEAD_DIM, 0.0).astype(BF16)

    def chunk_rows(c):
        return pl.ds(pl.multiple_of(c * CHUNK, CHUNK), CHUNK)

    def kv_update(kc, vc, zeta):
        kz = (kc.astype(F32) * zeta).astype(BF16)
        upd = lax.dot_general(kz, vc, TN_DIMS, preferred_element_type=F32)
        return jnp.where(same_head, upd, 0.0)

    def state_body(i, carry):
        sf, sb = carry
        cb = nc - 1 - i
        sf_ref[i] = sf.astype(BF16)
        sb_ref[cb] = sb.astype(BF16)
        rf = chunk_rows(i)
        rb = chunk_rows(cb)
        sf = sf * dg_f + kv_update(k_ref[rf, :], v_ref[rf, :], zeta_f)
        sb = sb * dg_b + kv_update(k_ref[rb, :], v_ref[rb, :], zeta_b)
        return sf, sb

    zero_state = jnp.zeros((LANES, LANES), F32)
    lax.fori_loop(0, nc, state_body, (zero_state, zero_state), unroll=RET_UNROLL)

    def score_body(c, carry):
        rows = chunk_rows(c)
        qc = q_ref[rows, :]
        zero = jnp.zeros_like(qc)
        q2 = jnp.concatenate([jnp.where(qk_head1, zero, qc), jnp.where(qk_head1, qc, zero)], axis=0)
        s = lax.dot_general(q2, k_ref[rows, :], NT_DIMS, preferred_element_type=F32)
        p = (s * decay2).astype(BF16)
        p_ref[c] = jnp.concatenate([p[:CHUNK], p[CHUNK:]], axis=1)
        return carry

    lax.fori_loop(0, nc, score_body, 0, unroll=RET_UNROLL)

    def value_body(c, carry):
        rows = chunk_rows(c)
        qc = q_ref[rows, :]
        vc = v_ref[rows, :]
        zero = jnp.zeros_like(vc)
        v2 = jnp.concatenate([jnp.where(v_head1, zero, vc), jnp.where(v_head1, vc, zero)], axis=0)
        o = jnp.dot(p_ref[c], v2, preferred_element_type=F32)
        o = o + xi_f * jnp.dot(qc, sf_ref[c], preferred_element_type=F32)
        o = o + xi_b * jnp.dot(qc, sb_ref[c], preferred_element_type=F32)
        acc_ref[rows, :] = o
        return carry

    lax.fori_loop(0, nc, value_body, 0, unroll=RET_UNROLL)

    def mean_body(b, carry):
        rows = pl.ds(pl.multiple_of(b * RET_NORM_ROWS, RET_NORM_ROWS), RET_NORM_ROWS)
        o = acc_ref[rows, :]
        acc_ref[rows, :] = o - _dot2(o, group_mean)
        return carry

    def norm_body(b, carry):
        rows = pl.ds(pl.multiple_of(b * RET_NORM_ROWS, RET_NORM_ROWS), RET_NORM_ROWS)
        oc = acc_ref[rows, :]
        var = _dot2(oc * oc, group_mean)
        on = oc * lax.rsqrt(var + EPS) * gn_ref[...]
        o_ref[rows, :] = (_silu(og_ref[rows, :].astype(F32)) * on).astype(BF16)
        return carry

    lax.fori_loop(0, seq_len // RET_NORM_ROWS, mean_body, 0, unroll=RET_NORM_UNROLL)
    lax.fori_loop(0, seq_len // RET_NORM_ROWS, norm_body, 0, unroll=RET_NORM_UNROLL)


def _retention(q, k, v, og, lgf, lgb, gn, batch, seq_len):
    T = batch * seq_len
    blk = pl.BlockSpec((seq_len, LANES), lambda b, p: (b, p))
    smem = pl.BlockSpec(memory_space=pltpu.SMEM)
    est = 2 * 5 * seq_len * LANES * 2 + 4 * (seq_len // CHUNK) * LANES * LANES * 2 + seq_len * LANES * 4 + (8 << 20)
    return pl.pallas_call(
        functools.partial(_retention_kernel, seq_len=seq_len),
        grid=(batch, RET_WIDTH // LANES),
        in_specs=[smem, smem, blk, blk, blk, blk, pl.BlockSpec((1, LANES), lambda b, p: (0, p))],
        out_specs=blk,
        out_shape=jax.ShapeDtypeStruct((T, RET_WIDTH), BF16),
        scratch_shapes=[pltpu.VMEM((seq_len // CHUNK, LANES, LANES), BF16),
                        pltpu.VMEM((seq_len // CHUNK, LANES, LANES), BF16),
                        pltpu.VMEM((seq_len // CHUNK, CHUNK, 2 * CHUNK), BF16),
                        pltpu.VMEM((seq_len, LANES), F32)],
        compiler_params=pltpu.CompilerParams(
            dimension_semantics=("parallel", "parallel"), vmem_limit_bytes=_vmem_limit(est)),
        name="retention",
    )(lgf, lgb, q, k, v, og, gn)


def _out_proj_kernel(a_ref, b_ref, h_ref, wa_ref, wb_ref, g_ref, wrh_ref, wrl_ref, xp_ref, hacc_ref, aff_ref):
    y = jnp.dot(a_ref[...], wa_ref[...], preferred_element_type=F32)
    y = y + jnp.dot(b_ref[...], wb_ref[...], preferred_element_type=F32)
    hn = h_ref[...] + y
    tm = hn.shape[0]
    for s in range(ROW_TILES):
        hacc_ref[pl.ds(s, tm, stride=ROW_TILES), :] = hn[:, s * LANES:(s + 1) * LANES]
    xn = _rmsnorm(hn, g_ref[...])
    x_hi = xn.astype(BF16)
    bits = pltpu.bitcast(x_hi.astype(F32), jnp.uint32)
    packed = (bits[:, D_MODEL // 2:] & jnp.uint32(0xFFFF0000)) | (bits[:, :D_MODEL // 2] >> 16)
    for s in range(PACK_TILES):
        xp_ref[pl.ds(s, tm, stride=PACK_TILES), :] = packed[:, s * LANES:(s + 1) * LANES]
    x_lo = (xn - x_hi.astype(F32)).astype(BF16)
    wr_hi = wrh_ref[...]
    logits = lax.dot_general(wr_hi, x_hi, NT_DIMS, preferred_element_type=F32)
    logits = logits + lax.dot_general(wr_hi, x_lo, NT_DIMS, preferred_element_type=F32)
    logits = logits + lax.dot_general(wrl_ref[...], x_hi, NT_DIMS, preferred_element_type=F32)
    e = jnp.exp(logits - jnp.max(logits, axis=0, keepdims=True))
    aff_ref[...] = e / jnp.sum(e, axis=0, keepdims=True)


def _out_proj(a, b, h, wa, wb, g, wr_hi, wr_lo, seq_len):
    T = h.shape[0]
    tm = _token_tile(seq_len)
    row = lambda i: (i, 0)
    const = lambda i: (0, 0)
    est = 2 * (2 * tm * CONV_CH * 2 + 3 * tm * D_MODEL * 4 + N_EXPERTS * tm * 4) + 2 * D_MODEL * D_MODEL * 2 \
        + 6 * tm * D_MODEL * 4
    return pl.pallas_call(
        _out_proj_kernel,
        grid=(T // tm,),
        in_specs=[
            pl.BlockSpec((tm, CONV_CH), row),
            pl.BlockSpec((tm, RET_WIDTH), row),
            pl.BlockSpec((tm, D_MODEL), row),
            pl.BlockSpec((CONV_CH, D_MODEL), const),
            pl.BlockSpec((RET_WIDTH, D_MODEL), const),
            pl.BlockSpec((1, D_MODEL), const),
            pl.BlockSpec((N_EXPERTS, D_MODEL), const),
            pl.BlockSpec((N_EXPERTS, D_MODEL), const),
        ],
        out_specs=[
            pl.BlockSpec((tm * PACK_TILES, LANES), row),
            pl.BlockSpec((tm * ROW_TILES, LANES), row),
            pl.BlockSpec((N_EXPERTS, tm), lambda i: (0, i)),
        ],
        out_shape=[
            jax.ShapeDtypeStruct((T * PACK_TILES, LANES), jnp.uint32),
            jax.ShapeDtypeStruct((T * ROW_TILES, LANES), F32),
            jax.ShapeDtypeStruct((N_EXPERTS, T), F32),
        ],
        compiler_params=pltpu.CompilerParams(
            dimension_semantics=("parallel",), vmem_limit_bytes=_vmem_limit(est)),
        name="out_proj",
    )(a, b, h, wa, wb, g, wr_hi, wr_lo)


def _select_kernel(aff_ref, idx_ref, gate_ref, *, cap):
    bits = pltpu.bitcast(aff_ref[0], jnp.int32)
    R = bits.shape[0]
    n_tok = R * LANES
    lane = lax.broadcasted_iota(jnp.int32, (R, LANES), 1)
    row = lax.broadcasted_iota(jnp.int32, (R, LANES), 0)

    def count(mask):
        ones_f = jnp.where(mask, 1.0, 0.0)
        return jnp.sum(jnp.sum(ones_f, axis=0, keepdims=True), axis=1, keepdims=True)

    thr = jnp.zeros((1, 1), jnp.int32)
    for bit in range(30, -1, -1):
        cand = thr | (1 << bit)
        thr = jnp.where(count(bits >= cand) >= cap, cand, thr)

    tri = (lax.broadcasted_iota(jnp.int32, (LANES, LANES), 0)
           <= lax.broadcasted_iota(jnp.int32, (LANES, LANES), 1)).astype(BF16)
    ones = jnp.ones((LANES, LANES), BF16)

    rows_before = (lax.broadcasted_iota(jnp.int32, (R, R), 1)
                   < lax.broadcasted_iota(jnp.int32, (R, R), 0)).astype(BF16)

    def exclusive_rank(mask):
        m = mask.astype(BF16)
        incl = jnp.dot(m, tri, preferred_element_type=F32)
        tot = jnp.dot(m, ones, preferred_element_type=F32).astype(BF16)
        before = jnp.dot(rows_before, tot, preferred_element_type=F32)
        return (before + incl).astype(jnp.int32) - mask.astype(jnp.int32)

    gt = bits > thr
    eq = bits == thr
    need = cap - count(gt)
    sel = gt | (eq & (exclusive_rank(eq) < need))
    tok = row * LANES + lane
    dist = jnp.where(sel, tok - exclusive_rank(sel), -1)

    def pull(x, a, fill):
        if a < LANES:
            near = pltpu.roll(x, LANES - a, 1)
            far = pltpu.roll(near, R - 1, 0) if R > 1 else near
            out = jnp.where(lane < LANES - a, near, far)
            valid = (row < R - 1) | (lane < LANES - a)
        else:
            s = a // LANES
            out = pltpu.roll(x, R - s, 0)
            valid = row < R - s
        return jnp.where(valid, out, fill)

    gbits = bits
    k = 0
    while (1 << k) < n_tok:
        a = 1 << k
        d_in = pull(dist, a, -1)
        t_in = pull(tok, a, 0)
        g_in = pull(gbits, a, 0)
        moves_in = (d_in >= 0) & (((d_in >> k) & 1) == 1)
        stays = (dist >= 0) & (((dist >> k) & 1) == 0)
        tok = jnp.where(moves_in, t_in, tok)
        gbits = jnp.where(moves_in, g_in, gbits)
        dist = jnp.where(moves_in, d_in, jnp.where(stays, dist, -1))
        k += 1

    rows_out = cap // LANES
    idx_ref[0] = tok[:rows_out]
    gate_ref[0] = pltpu.bitcast(gbits[:rows_out], F32)


def _select(aff_t, cap):
    T = aff_t.shape[1]
    R = T // LANES
    rows_out = cap // LANES
    aff3 = aff_t.reshape(N_EXPERTS, R, LANES)
    est = 40 * R * LANES * 4 + (2 << 20)
    idx, gate = pl.pallas_call(
        functools.partial(_select_kernel, cap=cap),
        grid=(N_EXPERTS,),
        in_specs=[pl.BlockSpec((1, R, LANES), lambda e: (e, 0, 0))],
        out_specs=[pl.BlockSpec((1, rows_out, LANES), lambda e: (e, 0, 0)),
                   pl.BlockSpec((1, rows_out, LANES), lambda e: (e, 0, 0))],
        out_shape=[jax.ShapeDtypeStruct((N_EXPERTS, rows_out, LANES), jnp.int32),
                   jax.ShapeDtypeStruct((N_EXPERTS, rows_out, LANES), F32)],
        compiler_params=pltpu.CompilerParams(
            dimension_semantics=("parallel",), vmem_limit_bytes=_vmem_limit(est)),
        name="select",
    )(aff3)
    return idx.reshape(N_EXPERTS, cap), gate.reshape(N_EXPERTS, cap)


FF_CHUNK = 512


MOE_PITCH = ROW_TILES + 1
MOE_XPITCH = PACK_TILES + 1
MOE_PIECES = 6
GATE_ROWS = 512


def _moe_kernel(pidx_ref, idx_ref, nidx_ref, gate_ref, wg_ref, wu_ref, wd_ref, xp_hbm, acc_in_hbm,
                acc_hbm, xbuf, abuf, obuf, xn_ref, xsem, asem, osem, *, tile):
    del acc_in_hbm
    nt = pl.num_programs(1)
    step = pl.program_id(0) * nt + pl.program_id(1)
    last = pl.num_programs(0) * nt - 1
    slot = step % 2
    other = 1 - slot

    def token_rows(ref, t):
        return ref.at[pl.ds(pl.multiple_of(t * ROW_TILES, ROW_TILES), ROW_TILES), :]

    def buf_rows(j):
        return pl.ds(j * MOE_PITCH, ROW_TILES)

    def x_copy(ids, j, sl):
        src = xp_hbm.at[pl.ds(pl.multiple_of(ids[0, 0, j] * PACK_TILES, PACK_TILES), PACK_TILES), :]
        return pltpu.make_async_copy(src, xbuf.at[sl, pl.ds(j * MOE_XPITCH, PACK_TILES), :], xsem.at[sl])

    def a_copy(j):
        return pltpu.make_async_copy(token_rows(acc_hbm, idx_ref[0, 0, j]), abuf.at[buf_rows(j), :], asem)

    def o_copy(ids, j, sl):
        return pltpu.make_async_copy(obuf.at[sl, buf_rows(j), :], token_rows(acc_hbm, ids[0, 0, j]), osem.at[sl])

    def seed_copy(j):
        return pltpu.make_async_copy(token_rows(acc_hbm, idx_ref[0, 0, j]), obuf.at[other, buf_rows(j), :],
                                     osem.at[other])

    def for_rows(fn):
        def body(j, c):
            fn(j)
            return c
        lax.fori_loop(0, tile, body, 0)

    @pl.when(step == 0)
    def _():
        for_rows(lambda j: x_copy(idx_ref, j, slot).start())
        for_rows(lambda j: seed_copy(j).start())
        for_rows(lambda j: seed_copy(j).wait())

    for j in range(tile):
        x_copy(idx_ref, j, slot).wait()
    for s in range(PACK_TILES):
        w = xbuf[slot, pl.ds(s, tile, stride=MOE_XPITCH), :]
        lo = pltpu.bitcast(w << 16, F32)
        hi = pltpu.bitcast(w & jnp.uint32(0xFFFF0000), F32)
        xn_ref[:, s * LANES:(s + 1) * LANES] = lo.astype(BF16)
        xn_ref[:, D_MODEL // 2 + s * LANES:D_MODEL // 2 + (s + 1) * LANES] = hi.astype(BF16)

    issue_plan = {
        0: [functools.partial(lambda j: o_copy(pidx_ref, j, other).start(priority=j % 2), j) for j in range(tile)],
        1: [functools.partial(lambda j: x_copy(nidx_ref, j, other).start(priority=j % 2), j) for j in range(tile)],
        2: [functools.partial(lambda j: a_copy(j).start(priority=j % 2), j) for j in range(tile)],
    }
    half_ff = FF_CHUNK // 2
    half_d = D_MODEL // 2
    n_ff = EXPERT_FF // FF_CHUNK
    y0 = jnp.zeros((tile, half_d), F32)
    y1 = jnp.zeros((tile, half_d), F32)
    for c in range(n_ff):
        batch = issue_plan.get(c, [])
        per = -(-len(batch) // MOE_PIECES)

        def issue(k):
            for fn in batch[k * per:(k + 1) * per]:
                fn()

        lo = c * FF_CHUNK
        mid = lo + half_ff
        hi = lo + FF_CHUNK
        issue(0)
        hg0 = jnp.dot(xn_ref[...], wg_ref[0, :, lo:mid], preferred_element_type=F32)
        issue(1)
        hg1 = jnp.dot(xn_ref[...], wg_ref[0, :, mid:hi], preferred_element_type=F32)
        issue(2)
        hu0 = jnp.dot(xn_ref[...], wu_ref[0, :, lo:mid], preferred_element_type=F32)
        issue(3)
        hu1 = jnp.dot(xn_ref[...], wu_ref[0, :, mid:hi], preferred_element_type=F32)
        hid = jnp.concatenate([_silu(hg0) * hu0, _silu(hg1) * hu1], axis=1).astype(BF16)
        issue(4)
        y0 = y0 + jnp.dot(hid, wd_ref[0, lo:hi, :half_d], preferred_element_type=F32)
        issue(5)
        y1 = y1 + jnp.dot(hid, wd_ref[0, lo:hi, half_d:], preferred_element_type=F32)
        if c == 1:
            for j in range(tile):
                o_copy(pidx_ref, j, other).wait()
    gparts = []
    for r0 in range(0, tile, GATE_ROWS):
        rr = min(GATE_ROWS, tile - r0)
        eye = (lax.broadcasted_iota(jnp.int32, (rr, tile), 0) + r0) == lax.broadcasted_iota(jnp.int32, (rr, tile), 1)
        gparts.append(jnp.sum(jnp.where(eye, gate_ref[0], 0.0), axis=1, keepdims=True))
    gcol = gparts[0] if len(gparts) == 1 else jnp.concatenate(gparts, axis=0)

    for j in range(tile):
        a_copy(j).wait()
    for s in range(ROW_TILES):
        y = y0 if s < ROW_TILES // 2 else y1
        col = (s % (ROW_TILES // 2)) * LANES
        rows = pl.ds(s, tile, stride=MOE_PITCH)
        obuf[slot, rows, :] = abuf[rows, :] + y[:, col:col + LANES] * gcol

    @pl.when(step == last)
    def _():
        for_rows(lambda j: o_copy(idx_ref, j, slot).start())
        for_rows(lambda j: o_copy(idx_ref, j, slot).wait())
        for_rows(lambda j: x_copy(nidx_ref, j, other).wait())


def _moe(xp, hacc, idx, gate, experts, layer, tile):
    wg, wu, wd = experts
    cap = idx.shape[1]
    nt = cap // tile
    idx3 = idx.reshape(N_EXPERTS * nt, 1, tile)
    gate3 = gate.reshape(N_EXPERTS * nt, 1, tile)
    n_steps = N_EXPERTS * nt
    smem_idx = lambda im: pl.BlockSpec((1, 1, tile), im, memory_space=pltpu.SMEM)
    wspec = lambda shape: pl.BlockSpec((None, 1) + shape, lambda e, i: (layer, e, 0, 0),
                                       pipeline_mode=pl.Buffered(1))
    any_spec = pl.BlockSpec(memory_space=pl.ANY)
    buf_bytes = tile * MOE_PITCH * LANES * 4
    est = 2 * 3 * D_MODEL * EXPERT_FF * 2 + 5 * buf_bytes + tile * D_MODEL * 2 + 8 * tile * D_MODEL * 4 \
        + 6 * tile * FF_CHUNK * 4 + tile * tile * 4 * 2 + (2 << 20)
    return pl.pallas_call(
        functools.partial(_moe_kernel, tile=tile),
        grid=(N_EXPERTS, nt),
        in_specs=[
            smem_idx(lambda e, i: (jnp.maximum(e * nt + i - 1, 0), 0, 0)),
            smem_idx(lambda e, i: (e * nt + i, 0, 0)),
            smem_idx(lambda e, i: (jnp.minimum(e * nt + i + 1, n_steps - 1), 0, 0)),
            pl.BlockSpec((1, 1, tile), lambda e, i: (e * nt + i, 0, 0)),
            wspec((D_MODEL, EXPERT_FF)),
            wspec((D_MODEL, EXPERT_FF)),
            wspec((EXPERT_FF, D_MODEL)),
            any_spec,
            any_spec,
        ],
        out_specs=any_spec,
        out_shape=jax.ShapeDtypeStruct(hacc.shape, F32),
        scratch_shapes=[
            pltpu.VMEM((2, tile * MOE_XPITCH, LANES), jnp.uint32),
            pltpu.VMEM((tile * MOE_PITCH, LANES), F32),
            pltpu.VMEM((2, tile * MOE_PITCH, LANES), F32),
            pltpu.VMEM((tile, D_MODEL), BF16),
            pltpu.SemaphoreType.DMA((2,)),
            pltpu.SemaphoreType.DMA(()),
            pltpu.SemaphoreType.DMA((2,)),
        ],
        input_output_aliases={8: 0},
        compiler_params=pltpu.CompilerParams(
            dimension_semantics=("arbitrary", "arbitrary"), vmem_limit_bytes=_vmem_limit(est)),
        name="moe",
    )(idx3, idx3, idx3, gate3, wg, wu, wd, xp, hacc)


def _ple_kernel(h_ref, p_ref, g_ref, wg_ref, wp_ref, gf_ref, o_ref, *, final):
    tm = o_ref.shape[0]
    h = jnp.concatenate([h_ref[pl.ds(s, tm, stride=ROW_TILES), :] for s in range(ROW_TILES)], axis=1)
    xn = _rmsnorm(h, g_ref[...]).astype(BF16)
    gate = _sigmoid(jnp.dot(xn, wg_ref[...], preferred_element_type=F32))
    proj = jnp.dot(p_ref[...].astype(BF16), wp_ref[...], preferred_element_type=F32)
    out = h + gate * proj
    if final:
        out = _rmsnorm(out, gf_ref[...])
    o_ref[...] = out


def _ple(h, p, layer, g, wg, wp, gf, seq_len, final):
    T = h.shape[0] // ROW_TILES
    tm = _token_tile(seq_len)
    row = lambda i: (i, 0)
    const = lambda i: (0, 0)
    est = 2 * (2 * tm * D_MODEL * 4 + tm * PLE_DIM * 4) + 2 * (D_MODEL + PLE_DIM) * D_MODEL * 2 + 6 * tm * D_MODEL * 4
    return pl.pallas_call(
        functools.partial(_ple_kernel, final=final),
        grid=(T // tm,),
        in_specs=[
            pl.BlockSpec((tm * ROW_TILES, LANES), row),
            pl.BlockSpec((None, tm, PLE_DIM), lambda i: (layer, i, 0)),
            pl.BlockSpec((1, D_MODEL), const),
            pl.BlockSpec((D_MODEL, D_MODEL), const),
            pl.BlockSpec((PLE_DIM, D_MODEL), const),
            pl.BlockSpec((1, D_MODEL), const),
        ],
        out_specs=pl.BlockSpec((tm, D_MODEL), row),
        out_shape=jax.ShapeDtypeStruct((T, D_MODEL), F32),
        compiler_params=pltpu.CompilerParams(
            dimension_semantics=("parallel",), vmem_limit_bytes=_vmem_limit(est)),
        name="ple",
    )(h, p, g, wg, wp, gf)


def _qk_column_order():
    half = HEAD_DIM // 2
    order = []
    for pair in range(RET_HEADS // 2):
        for part in range(2):
            for hh in range(2):
                base = (2 * pair + hh) * HEAD_DIM + part * half
                order.extend(range(base, base + half))
    return jnp.asarray(order, jnp.int32)


def _rope_tables(seq_len):
    half = HEAD_DIM // 2
    inv = 1.0 / (ROPE_BASE ** (jnp.arange(half, dtype=F32) / half))
    ang = jnp.arange(seq_len, dtype=F32)[:, None] * inv[None, :]
    cos, sin = jnp.cos(ang), jnp.sin(ang)
    return (jnp.concatenate([cos, cos, cos, cos], axis=1),
            jnp.concatenate([-sin, -sin, sin, sin], axis=1))


def _prep_layer(i, w_in, w_out, w_router, w_ple_gate, w_ple_proj):
    order = _qk_column_order()
    w = w_in[i]
    wq = w[:, SPLITS[0]:SPLITS[1]][:, order]
    wk = w[:, SPLITS[1]:SPLITS[2]][:, order]
    w_in_p = jnp.concatenate([w[:, :SPLITS[0]], wq, wk, w[:, SPLITS[2]:]], axis=1).astype(BF16)
    wr_t = w_router[i].T
    wr_hi = wr_t.astype(BF16)
    wr_lo = (wr_t - wr_hi.astype(F32)).astype(BF16)
    wo = w_out[i].astype(BF16)
    return dict(
        w_in=w_in_p, wo_a=wo[:CONV_CH], wo_b=wo[CONV_CH:], wr_hi=wr_hi, wr_lo=wr_lo,
        w_ple_gate=w_ple_gate[i].astype(BF16), w_ple_proj=w_ple_proj[i].astype(BF16))


def _moe_tile(cap, want=1024):
    t = want
    while cap % t:
        t //= 2
    return t


def _trunk(x, p, layers, vecs, experts, norm_final):
    batch, seq_len, _ = x.shape
    T = batch * seq_len
    cap = CAP_FACTOR * T // N_EXPERTS
    cos_t, sin_t = _rope_tables(seq_len)
    h = x.reshape(T, D_MODEL)
    depth = len(layers)
    p = p.reshape(depth, T, PLE_DIM)
    for i, (lw, lv) in enumerate(zip(layers, vecs)):
        u, q, k, v, og = _in_proj(h, lv["norm_mix"], lw["w_in"], cos_t, sin_t, seq_len)
        a_out = _conv(u, lv["conv_w"], lv["conv_b"], lv["conv_ln_g"], lv["conv_ln_b"], batch, seq_len)
        b_out = _retention(q, k, v, og, lv["lg_f"], lv["lg_b"], lv["gn_g"], batch, seq_len)
        xp, h_acc, aff_t = _out_proj(a_out, b_out, h, lw["wo_a"], lw["wo_b"], lv["norm_ffn"],
                                     lw["wr_hi"], lw["wr_lo"], seq_len)
        idx, gate = _select(aff_t, cap)
        h_acc = _moe(xp, h_acc, idx, gate, experts, i, _moe_tile(cap))
        h = _ple(h_acc, p, i, lv["norm_ple"], lw["w_ple_gate"], lw["w_ple_proj"],
                 norm_final, seq_len, final=(i == depth - 1))
    return h.reshape(batch, seq_len, D_MODEL)


def kernel(x_prompt, x_sample, p_prompt, p_sample, norm_mix, w_in, conv_w, conv_b, conv_ln_g, conv_ln_b,
           ret_log_gamma_fwd, ret_log_gamma_bwd, ret_gn_g, w_out, norm_ffn, w_router, w_exp_gate, w_exp_up,
           w_exp_down, norm_ple, w_ple_gate, w_ple_proj, norm_final):
    depth = w_in.shape[0]
    layers = [_prep_layer(i, w_in, w_out, w_router, w_ple_gate, w_ple_proj) for i in range(depth)]
    experts = (w_exp_gate.astype(BF16), w_exp_up.astype(BF16), w_exp_down.astype(BF16))
    vecs = [dict(
        norm_mix=norm_mix[i].reshape(1, D_MODEL), conv_w=conv_w[i], conv_b=conv_b[i].reshape(1, CONV_CH),
        conv_ln_g=conv_ln_g[i].reshape(1, CONV_CH), conv_ln_b=conv_ln_b[i].reshape(1, CONV_CH),
        lg_f=ret_log_gamma_fwd[i], lg_b=ret_log_gamma_bwd[i], gn_g=ret_gn_g[i].reshape(1, RET_WIDTH),
        norm_ffn=norm_ffn[i].reshape(1, D_MODEL), norm_ple=norm_ple[i].reshape(1, D_MODEL))
        for i in range(depth)]
    gf = norm_final.reshape(1, D_MODEL)
    y_prompt = _trunk(x_prompt, p_prompt, layers, vecs, experts, gf)
    y_sample = _trunk(x_sample, p_sample, layers, vecs, experts, gf)
    return (y_prompt, y_sample)
```

```python
import functools
import math

import jax
import jax.numpy as jnp
from jax import lax
from jax.experimental import pallas as pl
from jax.experimental.pallas import tpu as pltpu

D_MODEL = 1024
CONV_CH = 512
RET_HEADS = 8
HEAD_DIM = 64
RET_WIDTH = RET_HEADS * HEAD_DIM
CONV_WIDTH = 31
CONV_PAD = CONV_WIDTH // 2
CHUNK = 128
ROPE_BASE = 10000.0
N_EXPERTS = 16
CAP_FACTOR = 2
EXPERT_FF = 2 * D_MODEL
PLE_DIM = 256
EPS = 1e-6
SPLITS = (2 * CONV_CH, 2 * CONV_CH + RET_WIDTH, 2 * CONV_CH + 2 * RET_WIDTH,
          2 * CONV_CH + 3 * RET_WIDTH, 2 * CONV_CH + 4 * RET_WIDTH)

LANES = 128
SUBLANES = 8
V7X_VMEM_BYTES = 64 * 1024 * 1024
V7X_VMEM_USABLE = 56 * 1024 * 1024
ROW_TILES = D_MODEL // LANES
PACK_TILES = ROW_TILES // 2

BF16 = jnp.bfloat16
F32 = jnp.float32
NT_DIMS = (((1,), (1,)), ((), ()))
TN_DIMS = (((0,), (0,)), ((), ()))


def _vmem_limit(estimate_bytes):
    return int(min(V7X_VMEM_USABLE, max(16 * 1024 * 1024, estimate_bytes)))


def _token_tile(seq_len, want=1024):
    tm = want
    while seq_len % tm:
        tm //= 2
    return tm


def _rmsnorm(x, g):
    y = x * lax.rsqrt(jnp.mean(x * x, axis=-1, keepdims=True) + EPS)
    return y * g


def _silu(x):
    return x * (1.0 / (1.0 + jnp.exp(-x)))


def _sigmoid(x):
    return 1.0 / (1.0 + jnp.exp(-x))


def _in_proj_kernel(h_ref, g_ref, w_ref, cq_ref, sq_ref, u_ref, q_ref, k_ref, v_ref, og_ref):
    xn = _rmsnorm(h_ref[...], g_ref[...]).astype(BF16)

    def seg(lo, hi):
        return jnp.dot(xn, w_ref[:, lo:hi], preferred_element_type=F32)

    u_ref[...] = seg(0, SPLITS[0]).astype(BF16)
    cos = cq_ref[...]
    sin = sq_ref[...]
    scale = HEAD_DIM ** -0.5
    for lo, ref, mul in ((SPLITS[0], q_ref, 1.0), (SPLITS[1], k_ref, scale)):
        full = seg(lo, lo + RET_WIDTH)
        for t in range(RET_WIDTH // LANES):
            x = full[:, t * LANES:(t + 1) * LANES]
            r = x * cos + pltpu.roll(x, LANES // 2, 1) * sin
            if mul != 1.0:
                r = r * mul
            ref[:, t * LANES:(t + 1) * LANES] = r.astype(BF16)
    v_ref[...] = seg(SPLITS[2], SPLITS[3]).astype(BF16)
    og_ref[...] = seg(SPLITS[3], SPLITS[4]).astype(BF16)


def _in_proj(h, g, w, cos_t, sin_t, seq_len):
    T = h.shape[0]
    tm = _token_tile(seq_len)
    nl = seq_len // tm
    row = lambda i: (i, 0)
    est = 2 * (tm * D_MODEL * 4 + tm * SPLITS[4] * 2 + 2 * tm * LANES * 4) + 2 * D_MODEL * SPLITS[4] * 2 \
        + 6 * tm * D_MODEL * 4
    return pl.pallas_call(
        _in_proj_kernel,
        grid=(T // tm,),
        in_specs=[
            pl.BlockSpec((tm, D_MODEL), row),
            pl.BlockSpec((1, D_MODEL), lambda i: (0, 0)),
            pl.BlockSpec((D_MODEL, SPLITS[4]), lambda i: (0, 0)),
            pl.BlockSpec((tm, LANES), lambda i: (i % nl, 0)),
            pl.BlockSpec((tm, LANES), lambda i: (i % nl, 0)),
        ],
        out_specs=[
            pl.BlockSpec((tm, 2 * CONV_CH), row),
            pl.BlockSpec((tm, RET_WIDTH), row),
            pl.BlockSpec((tm, RET_WIDTH), row),
            pl.BlockSpec((tm, RET_WIDTH), row),
            pl.BlockSpec((tm, RET_WIDTH), row),
        ],
        out_shape=[
            jax.ShapeDtypeStruct((T, 2 * CONV_CH), BF16),
            jax.ShapeDtypeStruct((T, RET_WIDTH), BF16),
            jax.ShapeDtypeStruct((T, RET_WIDTH), BF16),
            jax.ShapeDtypeStruct((T, RET_WIDTH), BF16),
            jax.ShapeDtypeStruct((T, RET_WIDTH), BF16),
        ],
        compiler_params=pltpu.CompilerParams(
            dimension_semantics=("parallel",), vmem_limit_bytes=_vmem_limit(est)),
        name="in_proj",
    )(h, g, w, cos_t, sin_t)


CONV_ROWS = 128
CONV_HALO = 16
CONV_WIN = CONV_ROWS + 2 * CONV_HALO


def _conv_kernel(u_ref, w_ref, b_ref, lg_ref, lb_ref, o_ref, hp_ref, cv_ref, *, seq_len):
    n_chunks = seq_len // CONV_ROWS
    zeros = jnp.zeros((CONV_HALO, CONV_CH), F32)
    hp_ref[0:CONV_HALO, :] = zeros
    hp_ref[CONV_HALO + seq_len:CONV_HALO + seq_len + CONV_HALO, :] = zeros

    def glu(ci, c):
        r0 = pl.multiple_of(ci * CONV_ROWS, CONV_ROWS)
        rows = u_ref[0, pl.ds(r0, CONV_ROWS), :].astype(F32)
        hp_ref[pl.ds(CONV_HALO + r0, CONV_ROWS), :] = rows[:, :CONV_CH] * _sigmoid(rows[:, CONV_CH:])
        return c

    lax.fori_loop(0, n_chunks, glu, 0)

    def conv(ci, c):
        r0 = pl.multiple_of(ci * CONV_ROWS, CONV_ROWS)
        for t in range(CONV_CH // LANES):
            cols = slice(t * LANES, (t + 1) * LANES)
            win = hp_ref[pl.ds(r0, CONV_WIN), cols]
            acc = jnp.zeros((CONV_ROWS, LANES), F32)
            for phase in range(SUBLANES):
                offs = [o for o in range(CONV_HALO - CONV_PAD, CONV_HALO - CONV_PAD + CONV_WIDTH)
                        if o % SUBLANES == phase]
                if not offs:
                    continue
                shifted = win if phase == 0 else pltpu.roll(win, CONV_WIN - phase, 0)
                for o in offs:
                    j = o - (CONV_HALO - CONV_PAD)
                    base = o - phase
                    acc = acc + shifted[base:base + CONV_ROWS] * w_ref[j:j + 1, cols]
            cv_ref[:, cols] = acc + b_ref[:, cols]
        y = cv_ref[...]
        mu = jnp.mean(y, axis=-1, keepdims=True)
        yc = y - mu
        var = jnp.mean(yc * yc, axis=-1, keepdims=True)
        z = yc * lax.rsqrt(var + EPS) * lg_ref[...] + lb_ref[...]
        o_ref[0, pl.ds(r0, CONV_ROWS), :] = _silu(z).astype(BF16)
        return c

    lax.fori_loop(0, n_chunks, conv, 0)


def _conv(u, w, b, lg, lb, batch, seq_len):
    u3 = u.reshape(batch, seq_len, 2 * CONV_CH)
    est = 2 * (seq_len * 2 * CONV_CH * 2 + seq_len * CONV_CH * 2) + (seq_len + 2 * CONV_HALO) * CONV_CH * 4 \
        + 8 * CONV_ROWS * CONV_CH * 4 + (1 << 20)
    const = lambda i: (0, 0)
    out = pl.pallas_call(
        functools.partial(_conv_kernel, seq_len=seq_len),
        grid=(batch,),
        in_specs=[
            pl.BlockSpec((1, seq_len, 2 * CONV_CH), lambda i: (i, 0, 0)),
            pl.BlockSpec((CONV_WIDTH, CONV_CH), const),
            pl.BlockSpec((1, CONV_CH), const),
            pl.BlockSpec((1, CONV_CH), const),
            pl.BlockSpec((1, CONV_CH), const),
        ],
        out_specs=pl.BlockSpec((1, seq_len, CONV_CH), lambda i: (i, 0, 0)),
        out_shape=jax.ShapeDtypeStruct((batch, seq_len, CONV_CH), BF16),
        scratch_shapes=[
            pltpu.VMEM((seq_len + 2 * CONV_HALO, CONV_CH), F32),
            pltpu.VMEM((CONV_ROWS, CONV_CH), F32),
        ],
        compiler_params=pltpu.CompilerParams(
            dimension_semantics=("parallel",), vmem_limit_bytes=_vmem_limit(est)),
        name="conv",
    )(u3, w, b, lg, lb)
    return out.reshape(batch * seq_len, CONV_CH)


def _dot2(x, m):
    hi = x.astype(BF16)
    lo = (x - hi.astype(F32)).astype(BF16)
    return jnp.dot(hi, m, preferred_element_type=F32) + jnp.dot(lo, m, preferred_element_type=F32)


RET_UNROLL = 16
RET_NORM_ROWS = 512
RET_NORM_UNROLL = 4


def _retention_kernel(lgf_ref, lgb_ref, q_ref, k_ref, v_ref, og_ref, gn_ref, o_ref, sf_ref, sb_ref, p_ref,
                      acc_ref, *, seq_len):
    nc = seq_len // CHUNK
    pair = pl.program_id(1)
    lgf = (lgf_ref[2 * pair], lgf_ref[2 * pair + 1])
    lgb = (lgb_ref[2 * pair], lgb_ref[2 * pair + 1])

    lane = lax.broadcasted_iota(jnp.int32, (CHUNK, LANES), 1)
    row = lax.broadcasted_iota(jnp.int32, (CHUNK, LANES), 0)
    rowf = row.astype(F32)
    qk_head1 = ((lane // (HEAD_DIM // 2)) % 2) == 1
    v_head1 = lane >= HEAD_DIM
    krow_head1 = ((row // (HEAD_DIM // 2)) % 2) == 1
    same_head = krow_head1 == v_head1

    def per_lane(pairvals, head1):
        return jnp.where(head1, pairvals[1], pairvals[0])

    lgf_qk = per_lane(lgf, qk_head1)
    lgb_qk = per_lane(lgb, qk_head1)
    lgf_v = per_lane(lgf, v_head1)
    lgb_v = per_lane(lgb, v_head1)
    zeta_f = jnp.exp(lgf_qk * (CHUNK - 1.0 - rowf))
    zeta_b = jnp.exp(lgb_qk * rowf)
    xi_f = jnp.exp(lgf_v * (rowf + 1.0))
    xi_b = jnp.exp(lgb_v * (CHUNK - rowf))
    dg_f = jnp.exp(per_lane(lgf, krow_head1) * float(CHUNK))
    dg_b = jnp.exp(per_lane(lgb, krow_head1) * float(CHUNK))
    diff = (row - lane).astype(F32)
    decay = []
    for hh in range(2):
        fwd = jnp.exp(lgf[hh] * jnp.maximum(diff, 0.0))
        bwd = jnp.exp(lgb[hh] * jnp.maximum(-diff, 0.0))
        decay.append(jnp.where(diff >= 0.0, fwd, bwd))
    decay2 = jnp.concatenate(decay, axis=0)
    group_mean = jnp.where((row >= HEAD_DIM) == v_head1, 1.0 / HEAD_DIM, 0.0).astype(BF16)

    def chunk_rows(c):
        return pl.ds(pl.multiple_of(c * CHUNK, CHUNK), CHUNK)

    def kv_update(kc, vc, zeta):
        kz = (kc.astype(F32) * zeta).astype(BF16)
        upd = lax.dot_general(kz, vc, TN_DIMS, preferred_element_type=F32)
        return jnp.where(same_head, upd, 0.0)

    def state_body(i, carry):
        sf, sb = carry
        cb = nc - 1 - i
        sf_ref[i] = sf.astype(BF16)
        sb_ref[cb] = sb.astype(BF16)
        rf = chunk_rows(i)
        rb = chunk_rows(cb)
        sf = sf * dg_f + kv_update(k_ref[rf, :], v_ref[rf, :], zeta_f)
        sb = sb * dg_b + kv_update(k_ref[rb, :], v_ref[rb, :], zeta_b)
        return sf, sb

    zero_state = jnp.zeros((LANES, LANES), F32)
    lax.fori_loop(0, nc, state_body, (zero_state, zero_state), unroll=RET_UNROLL)

    def score_body(c, carry):
        rows = chunk_rows(c)
        qc = q_ref[rows, :]
        zero = jnp.zeros_like(qc)
        q2 = jnp.concatenate([jnp.where(qk_head1, zero, qc), jnp.where(qk_head1, qc, zero)], axis=0)
        s = lax.dot_general(q2, k_ref[rows, :], NT_DIMS, preferred_element_type=F32)
        p = (s * decay2).astype(BF16)
        p_ref[c] = jnp.concatenate([p[:CHUNK], p[CHUNK:]], axis=1)
        return carry

    lax.fori_loop(0, nc, score_body, 0, unroll=RET_UNROLL)

    def value_body(c, carry):
        rows = chunk_rows(c)
        qc = q_ref[rows, :]
        vc = v_ref[rows, :]
        zero = jnp.zeros_like(vc)
        v2 = jnp.concatenate([jnp.where(v_head1, zero, vc), jnp.where(v_head1, vc, zero)], axis=0)
        o = jnp.dot(p_ref[c], v2, preferred_element_type=F32)
        o = o + xi_f * jnp.dot(qc, sf_ref[c], preferred_element_type=F32)
        o = o + xi_b * jnp.dot(qc, sb_ref[c], preferred_element_type=F32)
        acc_ref[rows, :] = o
        return carry

    lax.fori_loop(0, nc, value_body, 0, unroll=RET_UNROLL)

    def mean_body(b, carry):
        rows = pl.ds(pl.multiple_of(b * RET_NORM_ROWS, RET_NORM_ROWS), RET_NORM_ROWS)
        o = acc_ref[rows, :]
        acc_ref[rows, :] = o - _dot2(o, group_mean)
        return carry

    def norm_body(b, carry):
        rows = pl.ds(pl.multiple_of(b * RET_NORM_ROWS, RET_NORM_ROWS), RET_NORM_ROWS)
        oc = acc_ref[rows, :]
        var = _dot2(oc * oc, group_mean)
        on = oc * lax.rsqrt(var + EPS) * gn_ref[...]
        o_ref[rows, :] = (_silu(og_ref[rows, :].astype(F32)) * on).astype(BF16)
        return carry

    lax.fori_loop(0, seq_len // RET_NORM_ROWS, mean_body, 0, unroll=RET_NORM_UNROLL)
    lax.fori_loop(0, seq_len // RET_NORM_ROWS, norm_body, 0, unroll=RET_NORM_UNROLL)


def _retention(q, k, v, og, lgf, lgb, gn, batch, seq_len):
    T = batch * seq_len
    blk = pl.BlockSpec((seq_len, LANES), lambda b, p: (b, p))
    smem = pl.BlockSpec(memory_space=pltpu.SMEM)
    est = 2 * 5 * seq_len * LANES * 2 + 4 * (seq_len // CHUNK) * LANES * LANES * 2 + seq_len * LANES * 4 + (8 << 20)
    return pl.pallas_call(
        functools.partial(_retention_kernel, seq_len=seq_len),
        grid=(batch, RET_WIDTH // LANES),
        in_specs=[smem, smem, blk, blk, blk, blk, pl.BlockSpec((1, LANES), lambda b, p: (0, p))],
        out_specs=blk,
        out_shape=jax.ShapeDtypeStruct((T, RET_WIDTH), BF16),
        scratch_shapes=[pltpu.VMEM((seq_len // CHUNK, LANES, LANES), BF16),
                        pltpu.VMEM((seq_len // CHUNK, LANES, LANES), BF16),
                        pltpu.VMEM((seq_len // CHUNK, CHUNK, 2 * CHUNK), BF16),
                        pltpu.VMEM((seq_len, LANES), F32)],
        compiler_params=pltpu.CompilerParams(
            dimension_semantics=("parallel", "parallel"), vmem_limit_bytes=_vmem_limit(est)),
        name="retention",
    )(lgf, lgb, q, k, v, og, gn)


def _out_proj_kernel(a_ref, b_ref, h_ref, wa_ref, wb_ref, g_ref, wrh_ref, wrl_ref, xp_ref, hacc_ref, aff_ref):
    y = jnp.dot(a_ref[...], wa_ref[...], preferred_element_type=F32)
    y = y + jnp.dot(b_ref[...], wb_ref[...], preferred_element_type=F32)
    hn = h_ref[...] + y
    tm = hn.shape[0]
    for s in range(ROW_TILES):
        hacc_ref[pl.ds(s, tm, stride=ROW_TILES), :] = hn[:, s * LANES:(s + 1) * LANES]
    xn = _rmsnorm(hn, g_ref[...])
    x_hi = xn.astype(BF16)
    bits = pltpu.bitcast(x_hi.astype(F32), jnp.uint32)
    packed = (bits[:, D_MODEL // 2:] & jnp.uint32(0xFFFF0000)) | (bits[:, :D_MODEL // 2] >> 16)
    for s in range(PACK_TILES):
        xp_ref[pl.ds(s, tm, stride=PACK_TILES), :] = packed[:, s * LANES:(s + 1) * LANES]
    x_lo = (xn - x_hi.astype(F32)).astype(BF16)
    wr_hi = wrh_ref[...]
    logits = lax.dot_general(wr_hi, x_hi, NT_DIMS, preferred_element_type=F32)
    logits = logits + lax.dot_general(wr_hi, x_lo, NT_DIMS, preferred_element_type=F32)
    logits = logits + lax.dot_general(wrl_ref[...], x_hi, NT_DIMS, preferred_element_type=F32)
    e = jnp.exp(logits - jnp.max(logits, axis=0, keepdims=True))
    aff_ref[...] = e / jnp.sum(e, axis=0, keepdims=True)


def _out_proj(a, b, h, wa, wb, g, wr_hi, wr_lo, seq_len):
    T = h.shape[0]
    tm = _token_tile(seq_len)
    row = lambda i: (i, 0)
    const = lambda i: (0, 0)
    est = 2 * (2 * tm * CONV_CH * 2 + 3 * tm * D_MODEL * 4 + N_EXPERTS * tm * 4) + 2 * D_MODEL * D_MODEL * 2 \
        + 6 * tm * D_MODEL * 4
    return pl.pallas_call(
        _out_proj_kernel,
        grid=(T // tm,),
        in_specs=[
            pl.BlockSpec((tm, CONV_CH), row),
            pl.BlockSpec((tm, RET_WIDTH), row),
            pl.BlockSpec((tm, D_MODEL), row),
            pl.BlockSpec((CONV_CH, D_MODEL), const),
            pl.BlockSpec((RET_WIDTH, D_MODEL), const),
            pl.BlockSpec((1, D_MODEL), const),
            pl.BlockSpec((N_EXPERTS, D_MODEL), const),
            pl.BlockSpec((N_EXPERTS, D_MODEL), const),
        ],
        out_specs=[
            pl.BlockSpec((tm * PACK_TILES, LANES), row),
            pl.BlockSpec((tm * ROW_TILES, LANES), row),
            pl.BlockSpec((N_EXPERTS, tm), lambda i: (0, i)),
        ],
        out_shape=[
            jax.ShapeDtypeStruct((T * PACK_TILES, LANES), jnp.uint32),
            jax.ShapeDtypeStruct((T * ROW_TILES, LANES), F32),
            jax.ShapeDtypeStruct((N_EXPERTS, T), F32),
        ],
        compiler_params=pltpu.CompilerParams(
            dimension_semantics=("parallel",), vmem_limit_bytes=_vmem_limit(est)),
        name="out_proj",
    )(a, b, h, wa, wb, g, wr_hi, wr_lo)


def _select_kernel(aff_ref, idx_ref, gate_ref, *, cap):
    bits = pltpu.bitcast(aff_ref[0], jnp.int32)
    R = bits.shape[0]
    n_tok = R * LANES
    lane = lax.broadcasted_iota(jnp.int32, (R, LANES), 1)
    row = lax.broadcasted_iota(jnp.int32, (R, LANES), 0)

    def count(mask):
        ones_f = jnp.where(mask, 1.0, 0.0)
        return jnp.sum(jnp.sum(ones_f, axis=0, keepdims=True), axis=1, keepdims=True)

    thr = jnp.zeros((1, 1), jnp.int32)
    for bit in range(30, -1, -1):
        cand = thr | (1 << bit)
        thr = jnp.where(count(bits >= cand) >= cap, cand, thr)

    tri = (lax.broadcasted_iota(jnp.int32, (LANES, LANES), 0)
           <= lax.broadcasted_iota(jnp.int32, (LANES, LANES), 1)).astype(BF16)
    ones = jnp.ones((LANES, LANES), BF16)

    rows_before = (lax.broadcasted_iota(jnp.int32, (R, R), 1)
                   < lax.broadcasted_iota(jnp.int32, (R, R), 0)).astype(BF16)

    def exclusive_rank(mask):
        m = mask.astype(BF16)
        incl = jnp.dot(m, tri, preferred_element_type=F32)
        tot = jnp.dot(m, ones, preferred_element_type=F32).astype(BF16)
        before = jnp.dot(rows_before, tot, preferred_element_type=F32)
        return (before + incl).astype(jnp.int32) - mask.astype(jnp.int32)

    gt = bits > thr
    eq = bits == thr
    need = cap - count(gt)
    sel = gt | (eq & (exclusive_rank(eq) < need))
    tok = row * LANES + lane
    dist = jnp.where(sel, tok - exclusive_rank(sel), -1)

    def pull(x, a, fill):
        if a < LANES:
            near = pltpu.roll(x, LANES - a, 1)
            far = pltpu.roll(near, R - 1, 0) if R > 1 else near
            out = jnp.where(lane < LANES - a, near, far)
            valid = (row < R - 1) | (lane < LANES - a)
        else:
            s = a // LANES
            out = pltpu.roll(x, R - s, 0)
            valid = row < R - s
        return jnp.where(valid, out, fill)

    gbits = bits
    k = 0
    while (1 << k) < n_tok:
        a = 1 << k
        d_in = pull(dist, a, -1)
        t_in = pull(tok, a, 0)
        g_in = pull(gbits, a, 0)
        moves_in = (d_in >= 0) & (((d_in >> k) & 1) == 1)
        stays = (dist >= 0) & (((dist >> k) & 1) == 0)
        tok = jnp.where(moves_in, t_in, tok)
        gbits = jnp.where(moves_in, g_in, gbits)
        dist = jnp.where(moves_in, d_in, jnp.where(stays, dist, -1))
        k += 1

    rows_out = cap // LANES
    idx_ref[0] = tok[:rows_out]
    gate_ref[0] = pltpu.bitcast(gbits[:rows_out], F32)


def _select(aff_t, cap):
    T = aff_t.shape[1]
    R = T // LANES
    rows_out = cap // LANES
    aff3 = aff_t.reshape(N_EXPERTS, R, LANES)
    est = 40 * R * LANES * 4 + (2 << 20)
    idx, gate = pl.pallas_call(
        functools.partial(_select_kernel, cap=cap),
        grid=(N_EXPERTS,),
        in_specs=[pl.BlockSpec((1, R, LANES), lambda e: (e, 0, 0))],
        out_specs=[pl.BlockSpec((1, rows_out, LANES), lambda e: (e, 0, 0)),
                   pl.BlockSpec((1, rows_out, LANES), lambda e: (e, 0, 0))],
        out_shape=[jax.ShapeDtypeStruct((N_EXPERTS, rows_out, LANES), jnp.int32),
                   jax.ShapeDtypeStruct((N_EXPERTS, rows_out, LANES), F32)],
        compiler_params=pltpu.CompilerParams(
            dimension_semantics=("parallel",), vmem_limit_bytes=_vmem_limit(est)),
        name="select",
    )(aff3)
    return idx.reshape(N_EXPERTS, cap), gate.reshape(N_EXPERTS, cap)


FF_CHUNK = 512


MOE_PITCH = ROW_TILES + 1
MOE_XPITCH = PACK_TILES + 1
MOE_PIECES = 6
GATE_ROWS = 512


def _moe_kernel(pidx_ref, idx_ref, nidx_ref, gate_ref, wg_ref, wu_ref, wd_ref, xp_hbm, acc_in_hbm,
                acc_hbm, xbuf, abuf, obuf, xn_ref, xsem, asem, osem, *, tile):
    del acc_in_hbm
    nt = pl.num_programs(1)
    step = pl.program_id(0) * nt + pl.program_id(1)
    last = pl.num_programs(0) * nt - 1
    slot = step % 2
    other = 1 - slot

    def token_rows(ref, t):
        return ref.at[pl.ds(pl.multiple_of(t * ROW_TILES, ROW_TILES), ROW_TILES), :]

    def buf_rows(j):
        return pl.ds(j * MOE_PITCH, ROW_TILES)

    def x_copy(ids, j, sl):
        src = xp_hbm.at[pl.ds(pl.multiple_of(ids[0, 0, j] * PACK_TILES, PACK_TILES), PACK_TILES), :]
        return pltpu.make_async_copy(src, xbuf.at[sl, pl.ds(j * MOE_XPITCH, PACK_TILES), :], xsem.at[sl])

    def a_copy(j):
        return pltpu.make_async_copy(token_rows(acc_hbm, idx_ref[0, 0, j]), abuf.at[buf_rows(j), :], asem)

    def o_copy(ids, j, sl):
        return pltpu.make_async_copy(obuf.at[sl, buf_rows(j), :], token_rows(acc_hbm, ids[0, 0, j]), osem.at[sl])

    def seed_copy(j):
        return pltpu.make_async_copy(token_rows(acc_hbm, idx_ref[0, 0, j]), obuf.at[other, buf_rows(j), :],
                                     osem.at[other])

    def for_rows(fn):
        def body(j, c):
            fn(j)
            return c
        lax.fori_loop(0, tile, body, 0)

    @pl.when(step == 0)
    def _():
        for_rows(lambda j: x_copy(idx_ref, j, slot).start())
        for_rows(lambda j: seed_copy(j).start())
        for_rows(lambda j: seed_copy(j).wait())

    for j in range(tile):
        x_copy(idx_ref, j, slot).wait()
    for s in range(PACK_TILES):
        w = xbuf[slot, pl.ds(s, tile, stride=MOE_XPITCH), :]
        lo = pltpu.bitcast(w << 16, F32)
        hi = pltpu.bitcast(w & jnp.uint32(0xFFFF0000), F32)
        xn_ref[:, s * LANES:(s + 1) * LANES] = lo.astype(BF16)
        xn_ref[:, D_MODEL // 2 + s * LANES:D_MODEL // 2 + (s + 1) * LANES] = hi.astype(BF16)

    issue_plan = {
        0: [functools.partial(lambda j: o_copy(pidx_ref, j, other).start(priority=j % 2), j) for j in range(tile)],
        1: [functools.partial(lambda j: x_copy(nidx_ref, j, other).start(priority=j % 2), j) for j in range(tile)],
        2: [functools.partial(lambda j: a_copy(j).start(priority=j % 2), j) for j in range(tile)],
    }
    half_ff = FF_CHUNK // 2
    half_d = D_MODEL // 2
    n_ff = EXPERT_FF // FF_CHUNK
    y0 = jnp.zeros((tile, half_d), F32)
    y1 = jnp.zeros((tile, half_d), F32)
    for c in range(n_ff):
        batch = issue_plan.get(c, [])
        per = -(-len(batch) // MOE_PIECES)

        def issue(k):
            for fn in batch[k * per:(k + 1) * per]:
                fn()

        lo = c * FF_CHUNK
        mid = lo + half_ff
        hi = lo + FF_CHUNK
        issue(0)
        hg0 = jnp.dot(xn_ref[...], wg_ref[0, :, lo:mid], preferred_element_type=F32)
        issue(1)
        hg1 = jnp.dot(xn_ref[...], wg_ref[0, :, mid:hi], preferred_element_type=F32)
        issue(2)
        hu0 = jnp.dot(xn_ref[...], wu_ref[0, :, lo:mid], preferred_element_type=F32)
        issue(3)
        hu1 = jnp.dot(xn_ref[...], wu_ref[0, :, mid:hi], preferred_element_type=F32)
        hid = jnp.concatenate([_silu(hg0) * hu0, _silu(hg1) * hu1], axis=1).astype(BF16)
        issue(4)
        y0 = y0 + jnp.dot(hid, wd_ref[0, lo:hi, :half_d], preferred_element_type=F32)
        issue(5)
        y1 = y1 + jnp.dot(hid, wd_ref[0, lo:hi, half_d:], preferred_element_type=F32)
        if c == 1:
            for j in range(tile):
                o_copy(pidx_ref, j, other).wait()
    gparts = []
    for r0 in range(0, tile, GATE_ROWS):
        rr = min(GATE_ROWS, tile - r0)
        eye = (lax.broadcasted_iota(jnp.int32, (rr, tile), 0) + r0) == lax.broadcasted_iota(jnp.int32, (rr, tile), 1)
        gparts.append(jnp.sum(jnp.where(eye, gate_ref[0], 0.0), axis=1, keepdims=True))
    gcol = gparts[0] if len(gparts) == 1 else jnp.concatenate(gparts, axis=0)

    for j in range(tile):
        a_copy(j).wait()
    for s in range(ROW_TILES):
        y = y0 if s < ROW_TILES // 2 else y1
        col = (s % (ROW_TILES // 2)) * LANES
        rows = pl.ds(s, tile, stride=MOE_PITCH)
        obuf[slot, rows, :] = abuf[rows, :] + y[:, col:col + LANES] * gcol

    @pl.when(step == last)
    def _():
        for_rows(lambda j: o_copy(idx_ref, j, slot).start())
        for_rows(lambda j: o_copy(idx_ref, j, slot).wait())
        for_rows(lambda j: x_copy(nidx_ref, j, other).wait())


def _moe(xp, hacc, idx, gate, experts, layer, tile):
    wg, wu, wd = experts
    cap = idx.shape[1]
    nt = cap // tile
    idx3 = idx.reshape(N_EXPERTS * nt, 1, tile)
    gate3 = gate.reshape(N_EXPERTS * nt, 1, tile)
    n_steps = N_EXPERTS * nt
    smem_idx = lambda im: pl.BlockSpec((1, 1, tile), im, memory_space=pltpu.SMEM)
    wspec = lambda shape: pl.BlockSpec((None, 1) + shape, lambda e, i: (layer, e, 0, 0),
                                       pipeline_mode=pl.Buffered(1))
    any_spec = pl.BlockSpec(memory_space=pl.ANY)
    buf_bytes = tile * MOE_PITCH * LANES * 4
    est = 2 * 3 * D_MODEL * EXPERT_FF * 2 + 5 * buf_bytes + tile * D_MODEL * 2 + 8 * tile * D_MODEL * 4 \
        + 6 * tile * FF_CHUNK * 4 + tile * tile * 4 * 2 + (2 << 20)
    return pl.pallas_call(
        functools.partial(_moe_kernel, tile=tile),
        grid=(N_EXPERTS, nt),
        in_specs=[
            smem_idx(lambda e, i: (jnp.maximum(e * nt + i - 1, 0), 0, 0)),
            smem_idx(lambda e, i: (e * nt + i, 0, 0)),
            smem_idx(lambda e, i: (jnp.minimum(e * nt + i + 1, n_steps - 1), 0, 0)),
            pl.BlockSpec((1, 1, tile), lambda e, i: (e * nt + i, 0, 0)),
            wspec((D_MODEL, EXPERT_FF)),
            wspec((D_MODEL, EXPERT_FF)),
            wspec((EXPERT_FF, D_MODEL)),
            any_spec,
            any_spec,
        ],
        out_specs=any_spec,
        out_shape=jax.ShapeDtypeStruct(hacc.shape, F32),
        scratch_shapes=[
            pltpu.VMEM((2, tile * MOE_XPITCH, LANES), jnp.uint32),
            pltpu.VMEM((tile * MOE_PITCH, LANES), F32),
            pltpu.VMEM((2, tile * MOE_PITCH, LANES), F32),
            pltpu.VMEM((tile, D_MODEL), BF16),
            pltpu.SemaphoreType.DMA((2,)),
            pltpu.SemaphoreType.DMA(()),
            pltpu.SemaphoreType.DMA((2,)),
        ],
        input_output_aliases={8: 0},
        compiler_params=pltpu.CompilerParams(
            dimension_semantics=("arbitrary", "arbitrary"), vmem_limit_bytes=_vmem_limit(est)),
        name="moe",
    )(idx3, idx3, idx3, gate3, wg, wu, wd, xp, hacc)


def _ple_kernel(h_ref, p_ref, g_ref, wg_ref, wp_ref, gf_ref, o_ref, *, final):
    tm = o_ref.shape[0]
    h = jnp.concatenate([h_ref[pl.ds(s, tm, stride=ROW_TILES), :] for s in range(ROW_TILES)], axis=1)
    xn = _rmsnorm(h, g_ref[...]).astype(BF16)
    gate = _sigmoid(jnp.dot(xn, wg_ref[...], preferred_element_type=F32))
    proj = jnp.dot(p_ref[...].astype(BF16), wp_ref[...], preferred_element_type=F32)
    out = h + gate * proj
    if final:
        out = _rmsnorm(out, gf_ref[...])
    o_ref[...] = out


def _ple(h, p, layer, g, wg, wp, gf, seq_len, final):
    T = h.shape[0] // ROW_TILES
    tm = _token_tile(seq_len)
    row = lambda i: (i, 0)
    const = lambda i: (0, 0)
    est = 2 * (2 * tm * D_MODEL * 4 + tm * PLE_DIM * 4) + 2 * (D_MODEL + PLE_DIM) * D_MODEL * 2 + 6 * tm * D_MODEL * 4
    return pl.pallas_call(
        functools.partial(_ple_kernel, final=final),
        grid=(T // tm,),
        in_specs=[
            pl.BlockSpec((tm * ROW_TILES, LANES), row),
            pl.BlockSpec((None, tm, PLE_DIM), lambda i: (layer, i, 0)),
            pl.BlockSpec((1, D_MODEL), const),
            pl.BlockSpec((D_MODEL, D_MODEL), const),
            pl.BlockSpec((PLE_DIM, D_MODEL), const),
            pl.BlockSpec((1, D_MODEL), const),
        ],
        out_specs=pl.BlockSpec((tm, D_MODEL), row),
        out_shape=jax.ShapeDtypeStruct((T, D_MODEL), F32),
        compiler_params=pltpu.CompilerParams(
            dimension_semantics=("parallel",), vmem_limit_bytes=_vmem_limit(est)),
        name="ple",
    )(h, p, g, wg, wp, gf)


def _qk_column_order():
    half = HEAD_DIM // 2
    order = []
    for pair in range(RET_HEADS // 2):
        for part in range(2):
            for hh in range(2):
                base = (2 * pair + hh) * HEAD_DIM + part * half
                order.extend(range(base, base + half))
    return jnp.asarray(order, jnp.int32)


def _rope_tables(seq_len):
    half = HEAD_DIM // 2
    inv = 1.0 / (ROPE_BASE ** (jnp.arange(half, dtype=F32) / half))
    ang = jnp.arange(seq_len, dtype=F32)[:, None] * inv[None, :]
    cos, sin = jnp.cos(ang), jnp.sin(ang)
    return (jnp.concatenate([cos, cos, cos, cos], axis=1),
            jnp.concatenate([-sin, -sin, sin, sin], axis=1))


def _prep_layer(i, w_in, w_out, w_router, w_ple_gate, w_ple_proj):
    order = _qk_column_order()
    w = w_in[i]
    wq = w[:, SPLITS[0]:SPLITS[1]][:, order]
    wk = w[:, SPLITS[1]:SPLITS[2]][:, order]
    w_in_p = jnp.concatenate([w[:, :SPLITS[0]], wq, wk, w[:, SPLITS[2]:]], axis=1).astype(BF16)
    wr_t = w_router[i].T
    wr_hi = wr_t.astype(BF16)
    wr_lo = (wr_t - wr_hi.astype(F32)).astype(BF16)
    wo = w_out[i].astype(BF16)
    return dict(
        w_in=w_in_p, wo_a=wo[:CONV_CH], wo_b=wo[CONV_CH:], wr_hi=wr_hi, wr_lo=wr_lo,
        w_ple_gate=w_ple_gate[i].astype(BF16), w_ple_proj=w_ple_proj[i].astype(BF16))


def _moe_tile(cap, want=1024):
    t = want
    while cap % t:
        t //= 2
    return t


def _trunk(x, p, layers, vecs, experts, norm_final):
    batch, seq_len, _ = x.shape
    T = batch * seq_len
    cap = CAP_FACTOR * T // N_EXPERTS
    cos_t, sin_t = _rope_tables(seq_len)
    h = x.reshape(T, D_MODEL)
    depth = len(layers)
    p = p.reshape(depth, T, PLE_DIM)
    for i, (lw, lv) in enumerate(zip(layers, vecs)):
        u, q, k, v, og = _in_proj(h, lv["norm_mix"], lw["w_in"], cos_t, sin_t, seq_len)
        a_out = _conv(u, lv["conv_w"], lv["conv_b"], lv["conv_ln_g"], lv["conv_ln_b"], batch, seq_len)
        b_out = _retention(q, k, v, og, lv["lg_f"], lv["lg_b"], lv["gn_g"], batch, seq_len)
        xp, h_acc, aff_t = _out_proj(a_out, b_out, h, lw["wo_a"], lw["wo_b"], lv["norm_ffn"],
                                     lw["wr_hi"], lw["wr_lo"], seq_len)
        idx, gate = _select(aff_t, cap)
        h_acc = _moe(xp, h_acc, idx, gate, experts, i, _moe_tile(cap))
        h = _ple(h_acc, p, i, lv["norm_ple"], lw["w_ple_gate"], lw["w_ple_proj"],
                 norm_final, seq_len, final=(i == depth - 1))
    return h.reshape(batch, seq_len, D_MODEL)


def kernel(x_prompt, x_sample, p_prompt, p_sample, norm_mix, w_in, conv_w, conv_b, conv_ln_g, conv_ln_b,
           ret_log_gamma_fwd, ret_log_gamma_bwd, ret_gn_g, w_out, norm_ffn, w_router, w_exp_gate, w_exp_up,
           w_exp_down, norm_ple, w_ple_gate, w_ple_proj, norm_final):
    depth = w_in.shape[0]
    layers = [_prep_layer(i, w_in, w_out, w_router, w_ple_gate, w_ple_proj) for i in range(depth)]
    experts = (w_exp_gate.astype(BF16), w_exp_up.astype(BF16), w_exp_down.astype(BF16))
    vecs = [dict(
        norm_mix=norm_mix[i].reshape(1, D_MODEL), conv_w=conv_w[i], conv_b=conv_b[i].reshape(1, CONV_CH),
        conv_ln_g=conv_ln_g[i].reshape(1, CONV_CH), conv_ln_b=conv_ln_b[i].reshape(1, CONV_CH),
        lg_f=ret_log_gamma_fwd[i], lg_b=ret_log_gamma_bwd[i], gn_g=ret_gn_g[i].reshape(1, RET_WIDTH),
        norm_ffn=norm_ffn[i].reshape(1, D_MODEL), norm_ple=norm_ple[i].reshape(1, D_MODEL))
        for i in range(depth)]
    gf = norm_final.reshape(1, D_MODEL)
    y_prompt = _trunk(x_prompt, p_prompt, layers, vecs, experts, gf)
    y_sample = _trunk(x_sample, p_sample, layers, vecs, experts, gf)
    return (y_prompt, y_sample)
```
